```python
import jax
import jax.numpy as jnp
from jax import lax
import numpy as np

D_MODEL = 1024
BATCH = 8
SEQ = 2048
DEPTH = 2

N_MEM = 256
EXPAND = 2
D_INNER = EXPAND * D_MODEL
D_XATTN = D_INNER // 4
XATTN_HEADS = 4
XATTN_HEAD_DIM = D_XATTN // XATTN_HEADS
D_MIX = D_INNER - D_XATTN

MLSTM_HEADS = 4
MLSTM_HEAD_DIM = D_MIX // MLSTM_HEADS
MLSTM_CONV = 4
QKV_BLOCK = 4
N_QKV_BLOCKS = D_MIX // QKV_BLOCK
MLSTM_CHUNK = 64
ML_IN_W = D_MIX + D_XATTN + D_INNER

RWKV_HEAD_DIM = 64
RWKV_HEADS = D_MIX // RWKV_HEAD_DIM
DECAY_RANK = 64
ICLR_RANK = 64
VRES_RANK = 32
GATE_RANK = 128
RW_SHIFT_W = 3 * D_MIX + DECAY_RANK + ICLR_RANK + VRES_RANK + GATE_RANK
RW_IN_W = RW_SHIFT_W + D_XATTN + D_INNER

N_MLSTM = (DEPTH + 1) // 2
N_RWKV = DEPTH // 2

RMS_EPS = 1e-6
MHLN_EPS = 1e-5
RWKV_GN_EPS = 64e-5
L2_EPS = 1e-12

kernel_name = 'hybrid_mlstm_rwkv7_memxattn'


def rms_norm(x, g):
    xf = x.astype(jnp.float32)
    y = xf * lax.rsqrt(jnp.mean(xf * xf, axis=-1, keepdims=True) + RMS_EPS)
    return (y * g.astype(jnp.float32)).astype(x.dtype)


def head_norm(x, n_heads, eps):
    xf = x.astype(jnp.float32).reshape(x.shape[:-1] + (n_heads, -1))
    mu = jnp.mean(xf, axis=-1, keepdims=True)
    var = jnp.mean(jnp.square(xf - mu), axis=-1, keepdims=True)
    return ((xf - mu) * lax.rsqrt(var + eps)).reshape(x.shape)


def causal_dwconv(x, w, b):
    k_w, c = w.shape
    y = lax.conv_general_dilated(x, w[:, None, :].astype(x.dtype), window_strides=(1,),
                                 padding=[(k_w - 1, 0)], dimension_numbers=('NWC', 'WIO', 'NWC'),
                                 feature_group_count=c)
    return y + b.astype(x.dtype)


def blockdiag_linear(x, w):
    xb = x.reshape(x.shape[:-1] + w.shape[:2])
    return jnp.einsum('bsni,nio->bsno', xb, w.astype(x.dtype)).reshape(x.shape)


def token_shift_mix(p, mu):
    prev = jnp.pad(p, ((0, 0), (1, 0), (0, 0)))[:, :-1]
    return p + (prev - p) * mu.astype(p.dtype)


def mlstm_chunkwise(q, k, v, i_pre, logf):
    bsz, nh, seq, dh = q.shape
    n_chunks = seq // MLSTM_CHUNK

    def chunks(t):
        t = t.reshape(t.shape[:2] + (n_chunks, MLSTM_CHUNK) + t.shape[3:])
        return jnp.moveaxis(t, 2, 0)

    causal = jnp.tril(jnp.ones((MLSTM_CHUNK, MLSTM_CHUNK), dtype=bool))

    def step(carry, inp):
        c_st, n_st, m_st = carry
        qb, kb, vb, ib, fb = inp
        b = jnp.cumsum(fb, axis=-1)
        d_mat = jnp.where(causal, b[..., :, None] - b[..., None, :] + ib[..., None, :], -jnp.inf)
        inter = b + m_st[..., None]
        m_t = jnp.maximum(jnp.max(d_mat, axis=-1), inter)
        s = jnp.einsum('bhtd,bhsd->bhts', qb, kb) * jnp.exp(d_mat - m_t[..., None])
        g_in = jnp.exp(inter - m_t)
        num = jnp.einsum('bhts,bhse->bhte', s, vb) + g_in[..., None] * jnp.einsum('bhtd,bhde->bhte', qb, c_st)
        den = jnp.sum(s, axis=-1) + g_in * jnp.einsum('bhtd,bhd->bht', qb, n_st)
        h = num / jnp.maximum(jnp.abs(den), jnp.exp(-m_t))[..., None]
        b_last = b[..., -1]
        w_s = b_last[..., None] - b + ib
        m_new = jnp.maximum(b_last + m_st, jnp.max(w_s, axis=-1))
        kw = kb * jnp.exp(w_s - m_new[..., None])[..., None]
        g_old = jnp.exp(b_last + m_st - m_new)
        c_new = g_old[..., None, None] * c_st + jnp.einsum('bhsd,bhse->bhde', kw, vb)
        n_new = g_old[..., None] * n_st + jnp.sum(kw, axis=2)
        return (c_new, n_new, m_new), h

    init = (jnp.zeros((bsz, nh, dh, dh), jnp.float32), jnp.zeros((bsz, nh, dh), jnp.float32),
            jnp.zeros((bsz, nh), jnp.float32))
    _, hs = lax.scan(step, init, (chunks(q), chunks(k), chunks(v), chunks(i_pre), chunks(logf)))
    return jnp.moveaxis(hs, 0, 2).reshape(bsz, nh, seq, dh)


def mlstm_mixer(u, conv_w, conv_b, wq, wk, wv, w_gate, b_gate, mhn_g, skip):
    bsz, seq, _ = u.shape
    xc = jax.nn.silu(causal_dwconv(u, conv_w, conv_b))
    q = blockdiag_linear(xc, wq)
    k = blockdiag_linear(xc, wk)
    v = blockdiag_linear(u, wv)
    gates = (jnp.concatenate([q, k, v], axis=-1) @ w_gate.astype(u.dtype)).astype(jnp.float32) + b_gate
    i_pre = jnp.moveaxis(gates[..., :MLSTM_HEADS], -1, 1)
    logf = jax.nn.log_sigmoid(jnp.moveaxis(gates[..., MLSTM_HEADS:], -1, 1))

    def heads(t):
        return t.astype(jnp.float32).reshape(bsz, seq, MLSTM_HEADS, MLSTM_HEAD_DIM).transpose(0, 2, 1, 3)

    h = mlstm_chunkwise(heads(q), heads(k) * (MLSTM_HEAD_DIM ** -0.5), heads(v), i_pre, logf)
    h = h.transpose(0, 2, 1, 3).reshape(bsz, seq, D_MIX)
    h = head_norm(h, MLSTM_HEADS, MHLN_EPS) * mhn_g + skip * xc.astype(jnp.float32)
    return h, v


def wkv7_scan(r, w, k, v, a, b):
    bsz, _, nh, n = r.shape

    def step(state, inp):
        r_t, w_t, k_t, v_t, a_t, b_t = inp
        sa = jnp.einsum('bhij,bhj->bhi', state, a_t)
        state = (state * w_t[..., None, :] + sa[..., :, None] * b_t[..., None, :]
                 + v_t[..., :, None] * k_t[..., None, :])
        return state, jnp.einsum('bhij,bhj->bhi', state, r_t)

    s0 = jnp.zeros((bsz, nh, n, n), jnp.float32)
    xs = tuple(jnp.moveaxis(t, 1, 0) for t in (r, w, k, v, a, b))
    _, out = lax.scan(step, s0, xs)
    return jnp.moveaxis(out, 0, 1)


def rwkv7_mixer(p, v_first, w_lora2, w0, a_lora2, a0, v_lora2, v0, g_lora2, k_k, k_a, r_k, lnx_g, lnx_b):
    p = p.astype(jnp.float32)
    bsz, seq, _ = p.shape
    cuts = [D_MIX, 2 * D_MIX, 3 * D_MIX, 3 * D_MIX + DECAY_RANK, 3 * D_MIX + DECAY_RANK + ICLR_RANK,
            3 * D_MIX + DECAY_RANK + ICLR_RANK + VRES_RANK]
    r, k, v, wl, al, vl, gl = jnp.split(p, cuts, axis=-1)
    logw = -jax.nn.softplus(-(w0 + jnp.tanh(wl) @ w_lora2)) - 0.5
    decay = jnp.exp(-jnp.exp(logw))
    a = jax.nn.sigmoid(a0 + al @ a_lora2)
    v = v + (v_first.astype(jnp.float32) - v) * jax.nn.sigmoid(v0 + vl @ v_lora2)
    g = jax.nn.sigmoid(gl) @ g_lora2

    def heads(t):
        return t.reshape(bsz, seq, RWKV_HEADS, RWKV_HEAD_DIM)

    kk = heads(k * k_k)
    kk = kk / jnp.maximum(jnp.sqrt(jnp.sum(kk * kk, axis=-1, keepdims=True)), L2_EPS)
    k = k * (1.0 + (a - 1.0) * k_a)
    rh, kh, vh, ah = heads(r), heads(k), heads(v), heads(a)
    out = wkv7_scan(rh, heads(decay), kh, vh, -kk, kk * ah)
    out = head_norm(out.reshape(bsz, seq, D_MIX), RWKV_HEADS, RWKV_GN_EPS) * lnx_g + lnx_b
    bonus = jnp.sum(rh * kh * r_k, axis=-1, keepdims=True) * vh
    return (out + bonus.reshape(bsz, seq, D_MIX)) * g


def memory_attention(qm, mem_n, w_kv):
    bsz, seq, _ = qm.shape
    km, vm = jnp.split(mem_n @ w_kv.astype(mem_n.dtype), 2, axis=-1)
    q = qm.reshape(bsz, seq, XATTN_HEADS, XATTN_HEAD_DIM)
    km = km.reshape(bsz, -1, XATTN_HEADS, XATTN_HEAD_DIM)
    vm = vm.reshape(bsz, -1, XATTN_HEADS, XATTN_HEAD_DIM)
    s = jnp.einsum('bshd,bmhd->bhsm', q, km).astype(jnp.float32) * (XATTN_HEAD_DIM ** -0.5)
    pr = jax.nn.softmax(s, axis=-1)
    o = jnp.einsum('bhsm,bmhd->bshd', pr, vm.astype(jnp.float32))
    return o.reshape(bsz, seq, D_XATTN)


def setup_inputs(seed: int = 0) -> dict:
    key = jax.random.key(seed)
    ks = iter(jax.random.split(key, 48))
    f32 = jnp.float32

    def nrm(shape, scale):
        return jax.random.normal(next(ks), shape, f32) * scale

    def gain(shape):
        return 1.0 + nrm(shape, 0.02)

    x = nrm((BATCH, SEQ, D_MODEL), 1.0)
    mem = nrm((BATCH, N_MEM, D_MODEL), 1.0)
    norm_g = gain((DEPTH, D_MODEL))
    mem_norm_g = gain((DEPTH, D_MODEL))
    mem_kv_w = nrm((DEPTH, D_MODEL, 2 * D_XATTN), D_MODEL ** -0.5)
    w_out = nrm((DEPTH, D_INNER, D_MODEL), D_INNER ** -0.5)
    ml_w_in = nrm((N_MLSTM, D_MODEL, ML_IN_W), D_MODEL ** -0.5)
    ml_conv_w = nrm((N_MLSTM, MLSTM_CONV, D_MIX), MLSTM_CONV ** -0.5)
    ml_conv_b = nrm((N_MLSTM, D_MIX), 0.01)
    ml_wq = nrm((N_MLSTM, N_QKV_BLOCKS, QKV_BLOCK, QKV_BLOCK), QKV_BLOCK ** -0.5)
    ml_wk = nrm((N_MLSTM, N_QKV_BLOCKS, QKV_BLOCK, QKV_BLOCK), QKV_BLOCK ** -0.5)
    ml_wv = nrm((N_MLSTM, N_QKV_BLOCKS, QKV_BLOCK, QKV_BLOCK), QKV_BLOCK ** -0.5)
    ml_w_gate = nrm((N_MLSTM, 3 * D_MIX, 2 * MLSTM_HEADS), (3 * D_MIX) ** -0.5)
    fgate_bias = jnp.linspace(3.0, 6.0, MLSTM_HEADS, dtype=f32)
    ml_b_gate = jnp.concatenate([nrm((N_MLSTM, MLSTM_HEADS), 0.1),
                                 fgate_bias[None] + nrm((N_MLSTM, MLSTM_HEADS), 0.1)], axis=-1)
    ml_mhn_g = gain((N_MLSTM, D_MIX))
    ml_skip = gain((N_MLSTM, D_MIX))
    rw_w_in = nrm((N_RWKV, D_MODEL, RW_IN_W), D_MODEL ** -0.5)
    rw_mu = jax.random.uniform(next(ks), (N_RWKV, RW_SHIFT_W), f32, 0.0, 1.0)
    rw_w_lora2 = nrm((N_RWKV, DECAY_RANK, D_MIX), 0.5 * DECAY_RANK ** -0.5)
    chan = jnp.linspace(0.0, 1.0, D_MIX, dtype=f32)
    rw_w0 = (-6.5 + 5.0 * chan ** 0.85)[None] + nrm((N_RWKV, D_MIX), 0.1)
    rw_a_lora2 = nrm((N_RWKV, ICLR_RANK, D_MIX), 0.5 * ICLR_RANK ** -0.5)
    rw_a0 = nrm((N_RWKV, D_MIX), 0.1)
    rw_v_lora2 = nrm((N_RWKV, VRES_RANK, D_MIX), 0.5 * VRES_RANK ** -0.5)
    rw_v0 = 1.0 + nrm((N_RWKV, D_MIX), 0.1)
    rw_g_lora2 = nrm((N_RWKV, GATE_RANK, D_MIX), GATE_RANK ** -0.5)
    rw_k_k = 0.85 + nrm((N_RWKV, D_MIX), 0.02)
    rw_k_a = 1.0 + nrm((N_RWKV, D_MIX), 0.02)
    rw_r_k = -0.04 + nrm((N_RWKV, RWKV_HEADS, RWKV_HEAD_DIM), 0.02)
    rw_lnx_g = gain((N_RWKV, D_MIX))
    rw_lnx_b = nrm((N_RWKV, D_MIX), 0.01)
    final_g = gain((D_MODEL,))
    return {'x': x, 'mem': mem, 'norm_g': norm_g, 'mem_norm_g': mem_norm_g, 'mem_kv_w': mem_kv_w,
            'w_out': w_out, 'ml_w_in': ml_w_in, 'ml_conv_w': ml_conv_w, 'ml_conv_b': ml_conv_b,
            'ml_wq': ml_wq, 'ml_wk': ml_wk, 'ml_wv': ml_wv, 'ml_w_gate': ml_w_gate, 'ml_b_gate': ml_b_gate,
            'ml_mhn_g': ml_mhn_g, 'ml_skip': ml_skip, 'rw_w_in': rw_w_in, 'rw_mu': rw_mu,
            'rw_w_lora2': rw_w_lora2, 'rw_w0': rw_w0, 'rw_a_lora2': rw_a_lora2, 'rw_a0': rw_a0,
            'rw_v_lora2': rw_v_lora2, 'rw_v0': rw_v0, 'rw_g_lora2': rw_g_lora2, 'rw_k_k': rw_k_k,
            'rw_k_a': rw_k_a, 'rw_r_k': rw_r_k, 'rw_lnx_g': rw_lnx_g, 'rw_lnx_b': rw_lnx_b,
            'final_g': final_g}


def reference(x, mem, norm_g, mem_norm_g, mem_kv_w, w_out, ml_w_in, ml_conv_w, ml_conv_b, ml_wq, ml_wk,
              ml_wv, ml_w_gate, ml_b_gate, ml_mhn_g, ml_skip, rw_w_in, rw_mu, rw_w_lora2, rw_w0,
              rw_a_lora2, rw_a0, rw_v_lora2, rw_v0, rw_g_lora2, rw_k_k, rw_k_a, rw_r_k, rw_lnx_g,
              rw_lnx_b, final_g):
    v_first = None
    for i in range(DEPTH):
        j = i // 2
        h = rms_norm(x, norm_g[i])
        mem_n = rms_norm(mem, mem_norm_g[i])
        if i % 2 == 0:
            proj = h @ ml_w_in[j]
            u = proj[..., :D_MIX]
            qm = proj[..., D_MIX:D_MIX + D_XATTN]
            z = proj[..., D_MIX + D_XATTN:]
            y_mix, v_l = mlstm_mixer(u, ml_conv_w[j], ml_conv_b[j], ml_wq[j], ml_wk[j], ml_wv[j],
                                     ml_w_gate[j], ml_b_gate[j], ml_mhn_g[j], ml_skip[j])
            if i == 0:
                v_first = v_l
        else:
            proj = h @ rw_w_in[j]
            p = token_shift_mix(proj[..., :RW_SHIFT_W], rw_mu[j])
            qm = proj[..., RW_SHIFT_W:RW_SHIFT_W + D_XATTN]
            z = proj[..., RW_SHIFT_W + D_XATTN:]
            y_mix = rwkv7_mixer(p, v_first, rw_w_lora2[j], rw_w0[j], rw_a_lora2[j], rw_a0[j],
                                rw_v_lora2[j], rw_v0[j], rw_g_lora2[j], rw_k_k[j], rw_k_a[j],
                                rw_r_k[j], rw_lnx_g[j], rw_lnx_b[j])
        y_mem = memory_attention(qm, mem_n, mem_kv_w[i])
        y = jnp.concatenate([y_mix, y_mem], axis=-1) * jax.nn.silu(z.astype(jnp.float32))
        x = x + y.astype(x.dtype) @ w_out[i]
    return rms_norm(x, final_g)
```

```python
import functools
import math

import jax
import jax.numpy as jnp
from jax import lax
from jax.experimental import pallas as pl
from jax.experimental.pallas import tpu as pltpu

F32 = jnp.float32
BF16 = jnp.bfloat16
HIGHEST = lax.Precision.HIGHEST

LANES = 128
MXU_DIM = 256
VMEM_LIMIT = 48 * 1024 * 1024

XATTN_HEADS = 4
XATTN_HEAD_DIM = 128
MLSTM_HEADS = 4
MLSTM_CONV = 4
QKV_BLOCK = 4
RWKV_HEAD_DIM = 64
DECAY_RANK, ICLR_RANK, VRES_RANK, GATE_RANK = 64, 64, 32, 128
LORA_SEG = 128

RMS_EPS = 1e-6
MHLN_EPS = 1e-5
RWKV_GN_EPS = 64e-5
L2_EPS = 1e-12

MLSTM_CHUNK = 256
WKV_CHUNK = 64


def _params(*sem):
    return pltpu.CompilerParams(dimension_semantics=sem, vmem_limit_bytes=VMEM_LIMIT)


def _sigmoid(x):
    return 1.0 / (1.0 + jnp.exp(-x))


def _log_sigmoid(x):
    return jnp.minimum(x, 0.0) - jnp.log1p(jnp.exp(-jnp.abs(x)))


def _dot(a, b):
    return jnp.dot(a, b, preferred_element_type=F32)


def _dot_nt(a, b):
    return lax.dot_general(a, b, (((1,), (1,)), ((), ())), preferred_element_type=F32)


def _dot_tn(a, b):
    return lax.dot_general(a, b, (((0,), (0,)), ((), ())), preferred_element_type=F32)


def _bf(x):
    return x.astype(BF16)


def _rms_matmul_body(x_ref, g_ref, w_ref, o_ref, h_ref):
    @pl.when(pl.program_id(1) == 0)
    def _():
        x = x_ref[...]
        ms = jnp.mean(x * x, axis=-1, keepdims=True)
        h_ref[...] = _bf(x * lax.rsqrt(ms + RMS_EPS) * g_ref[...])

    o_ref[...] = _dot(h_ref[...], w_ref[...]).astype(o_ref.dtype)


def _rms_matmul(x2d, g, w, *, tm, tn, out_dtype):
    m, d = x2d.shape
    n = w.shape[1]
    return pl.pallas_call(
        _rms_matmul_body,
        grid=(m // tm, n // tn),
        in_specs=[pl.BlockSpec((tm, d), lambda i, j: (i, 0)),
                  pl.BlockSpec((1, d), lambda i, j: (0, 0)),
                  pl.BlockSpec((d, tn), lambda i, j: (0, j))],
        out_specs=pl.BlockSpec((tm, tn), lambda i, j: (i, j)),
        out_shape=jax.ShapeDtypeStruct((m, n), out_dtype),
        scratch_shapes=[pltpu.VMEM((tm, d), BF16)],
        compiler_params=_params("parallel", "arbitrary"),
        name="rms_matmul",
    )(x2d, g.reshape(1, d), w)


def _mlstm_pre_body(u_ref, cw_ref, cb_ref, wqk_ref, wv_ref, wg_ref, wgt_ref, bgr_ref, bgc_ref,
                    xc_ref, q_ref, k_ref, v_ref, gcol_ref, grow_ref, ext_ref, *, ts, nblk):
    s = pl.program_id(1)

    @pl.when(s == 0)
    def _():
        ext_ref[0:8, :] = jnp.zeros((8, ext_ref.shape[1]), F32)

    @pl.when(s > 0)
    def _():
        ext_ref[0:8, :] = ext_ref[ts:ts + 8, :]

    u = u_ref[...]
    ext_ref[8:8 + ts, :] = u
    acc = cb_ref[...] + cw_ref[MLSTM_CONV - 1:MLSTM_CONV, :] * u
    for j in range(1, MLSTM_CONV):
        acc = acc + cw_ref[MLSTM_CONV - 1 - j:MLSTM_CONV - j, :] * ext_ref[8 - j:8 - j + ts, :]
    xc = acc * _sigmoid(acc)
    xc_ref[...] = xc

    xcb = _bf(xc)
    ub = _bf(u)
    for b in range(nblk):
        lo, hi = b * MXU_DIM, (b + 1) * MXU_DIM
        qk = _dot(xcb[:, lo:hi], wqk_ref[b])
        q_ref[:, lo:hi] = qk[:, :MXU_DIM]
        k_ref[:, lo:hi] = qk[:, MXU_DIM:]
        v_ref[:, lo:hi] = _dot(ub[:, lo:hi], wv_ref[b])

    qb, kb, vb = _bf(q_ref[...]), _bf(k_ref[...]), _bf(v_ref[...])
    gc = _dot(qb, wg_ref[0]) + _dot(kb, wg_ref[1]) + _dot(vb, wg_ref[2]) + bgr_ref[...]
    lane = lax.broadcasted_iota(jnp.int32, gc.shape, 1)
    gc = jnp.where(lane < MLSTM_HEADS, gc, _log_sigmoid(gc))
    gcol_ref[...] = gc[:, :2 * MLSTM_HEADS]

    gr = _dot_nt(wgt_ref[0], qb) + _dot_nt(wgt_ref[1], kb) + _dot_nt(wgt_ref[2], vb)
    gr = gr[:2 * MLSTM_HEADS, :] + bgc_ref[...]
    row = lax.broadcasted_iota(jnp.int32, gr.shape, 0)
    grow_ref[...] = jnp.where(row < MLSTM_HEADS, gr, _log_sigmoid(gr))


def _mlstm_pre(proj, conv_w, conv_b, wqk, wv, wg, wgt, bg, *, ts):
    bsz, seq, _ = proj.shape
    dmix = conv_w.shape[1]
    nblk = dmix // MXU_DIM
    ng = 2 * MLSTM_HEADS
    full = lambda shape: pl.BlockSpec(shape, lambda b, s: (0,) * len(shape))
    tile = pl.BlockSpec((None, ts, dmix), lambda b, s: (b, s, 0))
    act = jax.ShapeDtypeStruct((bsz, seq, dmix), F32)
    return pl.pallas_call(
        functools.partial(_mlstm_pre_body, ts=ts, nblk=nblk),
        grid=(bsz, seq // ts),
        in_specs=[tile, full(conv_w.shape), full((1, dmix)), full(wqk.shape), full(wv.shape),
                  full(wg.shape), full(wgt.shape), full((1, LANES)), full((ng, 1))],
        out_specs=[tile, tile, tile, tile,
                   pl.BlockSpec((None, ts, ng), lambda b, s: (b, s, 0)),
                   pl.BlockSpec((None, ng, ts), lambda b, s: (b, 0, s))],
        out_shape=[act, act, act, act,
                   jax.ShapeDtypeStruct((bsz, seq, ng), F32),
                   jax.ShapeDtypeStruct((bsz, ng, seq), F32)],
        scratch_shapes=[pltpu.VMEM((ts + 8, dmix), F32)],
        compiler_params=_params("parallel", "arbitrary"),
        name="mlstm_pre",
    )(proj, conv_w, conv_b.reshape(1, dmix), wqk, wv, wg, wgt,
      jnp.pad(bg, (0, LANES - ng)).reshape(1, LANES), bg.reshape(ng, 1))


def _mlstm_scan_body(q_ref, k_ref, v_ref, gcol_ref, grow_ref, h_ref, c_ref, n_ref, m_ref, *, chunk, dh):
    @pl.when(pl.program_id(1) == 0)
    def _():
        c_ref[...] = jnp.zeros(c_ref.shape, F32)
        n_ref[...] = jnp.zeros(n_ref.shape, F32)
        m_ref[...] = jnp.zeros(m_ref.shape, F32)

    gc = gcol_ref[...]
    gr = grow_ref[...]
    row = lax.broadcasted_iota(jnp.int32, (chunk, chunk), 0)
    col = lax.broadcasted_iota(jnp.int32, (chunk, chunk), 1)
    causal = col <= row
    tril = causal.astype(F32)
    bcol_all = jnp.dot(tril, gc, precision=HIGHEST, preferred_element_type=F32)
    brow_all = lax.dot_general(gr, tril, (((1,), (1,)), ((), ())), precision=HIGHEST,
                               preferred_element_type=F32)
    scale = dh ** -0.5
    for h in range(MLSTM_HEADS):
        lo, hi = h * dh, (h + 1) * dh
        i_col = gc[:, h:h + 1]
        b_col = bcol_all[:, MLSTM_HEADS + h:MLSTM_HEADS + h + 1]
        i_row = gr[h:h + 1, :]
        b_row = brow_all[MLSTM_HEADS + h:MLSTM_HEADS + h + 1, :]
        m_old = m_ref[h][0:1, 0:1]
        d_mat = jnp.where(causal, b_col - b_row + i_row, -jnp.inf)
        inter = b_col + m_old
        m_t = jnp.maximum(jnp.max(d_mat, axis=-1, keepdims=True), inter)
        qh = q_ref[:, lo:hi]
        kh = k_ref[:, lo:hi] * scale
        qb, kb, vb = _bf(qh), _bf(kh), _bf(v_ref[:, lo:hi])
        s = _dot_nt(qb, kb) * jnp.exp(d_mat - m_t)
        g_in = jnp.exp(inter - m_t)
        c_old = c_ref[h]
        n_old = n_ref[h]
        num = _dot(_bf(s), vb) + g_in * _dot(qb, _bf(c_old))
        den = jnp.sum(s, axis=-1, keepdims=True) + g_in * jnp.sum(qh * n_old, axis=-1, keepdims=True)
        h_ref[:, lo:hi] = num / jnp.maximum(jnp.abs(den), jnp.exp(-m_t))
        b_last = b_col[chunk - 1:chunk, :]
        w_s = b_last - b_col + i_col
        m_new = jnp.maximum(b_last + m_old, jnp.max(w_s, axis=0, keepdims=True))
        kw = kh * jnp.exp(w_s - m_new)
        g_old = jnp.exp(b_last + m_old - m_new)
        c_ref[h] = g_old * c_old + _dot_tn(_bf(kw), vb)
        n_ref[h] = g_old * n_old + jnp.sum(kw, axis=0, keepdims=True)
        m_ref[h] = jnp.broadcast_to(m_new, m_ref.shape[1:])


def _mlstm_scan(q, k, v, gcol, grow, *, chunk):
    bsz, seq, dmix = q.shape
    dh = dmix // MLSTM_HEADS
    ng = 2 * MLSTM_HEADS
    tile = pl.BlockSpec((None, chunk, dmix), lambda b, c: (b, c, 0))
    return pl.pallas_call(
        functools.partial(_mlstm_scan_body, chunk=chunk, dh=dh),
        grid=(bsz, seq // chunk),
        in_specs=[tile, tile, tile,
                  pl.BlockSpec((None, chunk, ng), lambda b, c: (b, c, 0)),
                  pl.BlockSpec((None, ng, chunk), lambda b, c: (b, 0, c))],
        out_specs=tile,
        out_shape=jax.ShapeDtypeStruct((bsz, seq, dmix), F32),
        scratch_shapes=[pltpu.VMEM((MLSTM_HEADS, dh, dh), F32),
                        pltpu.VMEM((MLSTM_HEADS, 1, dh), F32),
                        pltpu.VMEM((MLSTM_HEADS, 8, LANES), F32)],
        compiler_params=_params("parallel", "arbitrary"),
        name="mlstm_scan",
    )(q, k, v, gcol, grow)


def _silu(z):
    return z * _sigmoid(z)


def _memory_attention_into(y_ref, qm, kv_ref, z_attn, col0):
    dx = XATTN_HEADS * XATTN_HEAD_DIM
    for h in range(XATTN_HEADS):
        lo, hi = h * XATTN_HEAD_DIM, (h + 1) * XATTN_HEAD_DIM
        s = _dot_nt(_bf(qm[:, lo:hi]), kv_ref[:, lo:hi]) * (XATTN_HEAD_DIM ** -0.5)
        p = jnp.exp(s - jnp.max(s, axis=-1, keepdims=True))
        o = _dot(_bf(p), kv_ref[:, dx + lo:dx + hi]) / jnp.sum(p, axis=-1, keepdims=True)
        y_ref[:, col0 + lo:col0 + hi] = _bf(o * _silu(z_attn[:, lo:hi]))


def _mlstm_out_body(hs_ref, xc_ref, qm_ref, z_ref, x_ref, kv_ref, wo_ref, mg_ref, sk_ref,
                    o_ref, y_ref, *, dh):
    dmix = MLSTM_HEADS * dh
    for h in range(MLSTM_HEADS):
        lo, hi = h * dh, (h + 1) * dh
        seg = hs_ref[:, lo:hi]
        d = seg - jnp.mean(seg, axis=-1, keepdims=True)
        var = jnp.mean(d * d, axis=-1, keepdims=True)
        ymix = d * lax.rsqrt(var + MHLN_EPS) * mg_ref[:, lo:hi] + sk_ref[:, lo:hi] * xc_ref[:, lo:hi]
        y_ref[:, lo:hi] = _bf(ymix * _silu(z_ref[:, lo:hi]))
    _memory_attention_into(y_ref, qm_ref[...], kv_ref, z_ref[:, dmix:], dmix)
    o_ref[...] = x_ref[...] + _dot(y_ref[...], wo_ref[...])


def _mlstm_out(hs, xc, proj, x, kv, w_out, mhn_g, skip, *, tm):
    bsz, seq, dmix = hs.shape
    dmodel = x.shape[-1]
    dx = XATTN_HEADS * XATTN_HEAD_DIM
    dinner = dmix + dx
    nmem = kv.shape[1]
    full = lambda shape: pl.BlockSpec(shape, lambda b, s: (0,) * len(shape))
    tile = lambda w, j: pl.BlockSpec((None, tm, w), lambda b, s: (b, s, j))
    return pl.pallas_call(
        functools.partial(_mlstm_out_body, dh=dmix // MLSTM_HEADS),
        grid=(bsz, seq // tm),
        in_specs=[tile(dmix, 0), tile(dmix, 0),
                  tile(dx, dmix // dx), tile(dinner, 1), tile(dmodel, 0),
                  pl.BlockSpec((None, nmem, 2 * dx), lambda b, s: (b, 0, 0)),
                  full(w_out.shape), full((1, dmix)), full((1, dmix))],
        out_specs=tile(dmodel, 0),
        out_shape=jax.ShapeDtypeStruct((bsz, seq, dmodel), F32),
        scratch_shapes=[pltpu.VMEM((tm, dinner), BF16)],
        compiler_params=_params("parallel", "parallel"),
        name="mlstm_out",
    )(hs, xc, proj, proj, x, kv, w_out, mhn_g.reshape(1, dmix), skip.reshape(1, dmix))


def _split2_dot(x, e):
    hi = _bf(x)
    lo = _bf(x - hi.astype(F32))
    return _dot(hi, e) + _dot(lo, e)


def _head_sums(x, e_ref):
    nslab = x.shape[1] // LANES
    return jnp.concatenate([_split2_dot(x[:, i * LANES:(i + 1) * LANES], e_ref[...]) for i in range(nslab)],
                           axis=1)


def _token_shift(x, carry_ref, mu):
    tm = x.shape[0]
    prev = pltpu.roll(x, 1, axis=0)
    first = lax.broadcasted_iota(jnp.int32, (tm, 1), 0) == 0
    prev = jnp.where(first, carry_ref[7:8, :], prev)
    carry_ref[...] = x[tm - 8:tm, :]
    return x + (prev - x) * mu


def _rwkv_pre_body(pr_ref, pk_ref, pv_ref, pl_ref, vf_ref, mur_ref, muk_ref, muv_ref, mul_ref,
                   wl2_ref, vecs_ref, e_ref,
                   r_ref, lw_ref, k_ref, v_ref, kk_ref, ab_ref, g_ref, bonus_ref,
                   cr_ref, ck_ref, cv_ref, cl_ref):
    @pl.when(pl.program_id(1) == 0)
    def _():
        for c in (cr_ref, ck_ref, cv_ref, cl_ref):
            c[...] = jnp.zeros(c.shape, F32)

    r = _token_shift(pr_ref[...], cr_ref, mur_ref[...])
    k = _token_shift(pk_ref[...], ck_ref, muk_ref[...])
    v = _token_shift(pv_ref[...], cv_ref, muv_ref[...])
    lo = _token_shift(pl_ref[...], cl_ref, mul_ref[...])
    w0, a0, v0, k_k, k_a, r_k = (vecs_ref[i:i + 1, :] for i in range(6))

    seg = lambda i: lo[:, i * LORA_SEG:(i + 1) * LORA_SEG]
    d = w0 + _dot(_bf(jnp.tanh(seg(0))), wl2_ref[0])
    lw_ref[...] = -math.exp(-0.5) * _sigmoid(d)
    a = _sigmoid(a0 + _dot(_bf(seg(1)), wl2_ref[1]))
    v = v + (vf_ref[...] - v) * _sigmoid(v0 + _dot(_bf(seg(2)), wl2_ref[2]))
    g_ref[...] = _dot(_bf(_sigmoid(seg(3))), wl2_ref[3])

    kk = k * k_k
    kk = kk / jnp.maximum(jnp.sqrt(_head_sums(kk * kk, e_ref)), L2_EPS)
    k2 = k * (1.0 + (a - 1.0) * k_a)
    r_ref[...] = r
    k_ref[...] = k2
    v_ref[...] = v
    kk_ref[...] = kk
    ab_ref[...] = kk * a
    bonus_ref[...] = _head_sums(r * k2 * r_k, e_ref) * v


def _rwkv_pre(proj, v_first, mus, wl2, vecs, e2, *, tm, lora_block):
    bsz, seq, dmix = v_first.shape
    lw = 4 * LORA_SEG
    full = lambda shape: pl.BlockSpec(shape, lambda b, s: (0,) * len(shape))
    tile = lambda w, j: pl.BlockSpec((None, tm, w), lambda b, s: (b, s, j))
    act = jax.ShapeDtypeStruct((bsz, seq, dmix), F32)
    return pl.pallas_call(
        _rwkv_pre_body,
        grid=(bsz, seq // tm),
        in_specs=[tile(dmix, 0), tile(dmix, 1), tile(dmix, 2), tile(lw, lora_block), tile(dmix, 0),
                  full((1, dmix)), full((1, dmix)), full((1, dmix)), full((1, lw)),
                  full(wl2.shape), full(vecs.shape), full(e2.shape)],
        out_specs=[tile(dmix, 0)] * 8,
        out_shape=[act] * 8,
        scratch_shapes=[pltpu.VMEM((8, dmix), F32)] * 3 + [pltpu.VMEM((8, lw), F32)],
        compiler_params=_params("parallel", "arbitrary"),
        name="rwkv_pre",
    )(proj, proj, proj, proj, v_first, *mus, wl2, vecs, e2)


def _wkv7_body(r_ref, lw_ref, k_ref, v_ref, kk_ref, ab_ref, o_ref, h_ref, *, chunk, npair):
    @pl.when(pl.program_id(1) == 0)
    def _():
        h_ref[...] = jnp.zeros(h_ref.shape, F32)

    c2 = 2 * chunk
    row = lax.broadcasted_iota(jnp.int32, (chunk, chunk), 0)
    col = lax.broadcasted_iota(jnp.int32, (chunk, chunk), 1)
    tril = (col <= row).astype(F32)
    lw = lw_ref[...]
    g_inc = jnp.dot(tril, lw, precision=HIGHEST, preferred_element_type=F32)
    g_exc = g_inc - lw
    g_last = g_inc[chunk - 1:chunk, :]
    e_inc = jnp.exp(g_inc)
    e_neg = jnp.exp(-g_inc)
    e_end = jnp.exp(g_last - g_inc)
    kk = kk_ref[...]
    ab = ab_ref[...]
    kx = k_ref[...]
    a_t = -kk * jnp.exp(g_exc)
    b_t = ab * e_neg
    k_t = kx * e_neg
    r_t = r_ref[...] * e_inc
    b_e = ab * e_end
    k_e = kx * e_end
    gam_l = jnp.exp(g_last)

    lane_head = lax.broadcasted_iota(jnp.int32, (c2, LANES), 1) // RWKV_HEAD_DIM
    row_head = lax.broadcasted_iota(jnp.int32, (c2, LANES), 0) // chunk
    head_mask = lane_head == row_head
    ri = lax.broadcasted_iota(jnp.int32, (c2, c2), 0)
    ci = lax.broadcasted_iota(jnp.int32, (c2, c2), 1)
    same = (ri // chunk) == (ci // chunk)
    strict = same & (ci < ri)
    incl = same & (ci <= ri)
    eye = (ri == ci).astype(F32)
    lane_eye = (lax.broadcasted_iota(jnp.int32, (LANES, LANES), 0)
                == lax.broadcasted_iota(jnp.int32, (LANES, LANES), 1)).astype(F32)

    def stack(x):
        return jnp.where(head_mask, jnp.concatenate([x, x], axis=0), 0.0)

    def dup(x):
        return jnp.concatenate([x, x], axis=0)

    for p in range(npair):
        sl = slice(p * LANES, (p + 1) * LANES)
        a_s = _bf(stack(a_t[:, sl]))
        r_s = _bf(stack(r_t[:, sl]))
        b_d = _bf(dup(b_t[:, sl]))
        k_d = _bf(dup(k_t[:, sl]))
        v_s = _bf(stack(v_ref[:, sl]))
        d_ab = jnp.where(strict, _dot_nt(a_s, b_d), 0.0)
        d_ak = jnp.where(strict, _dot_nt(a_s, k_d), 0.0)
        d_rb = jnp.where(incl, _dot_nt(r_s, b_d), 0.0)
        d_rk = jnp.where(incl, _dot_nt(r_s, k_d), 0.0)
        t_inv = eye + d_ab
        d_pow = d_ab
        for _ in range(int(math.log2(chunk)) - 1):
            d_pb = _bf(d_pow)
            d_pow = _dot(d_pb, d_pb)
            t_inv = t_inv + _dot(_bf(d_pow), _bf(t_inv))
        t_b = _bf(t_inv)
        w_s = _dot(t_b, a_s)
        u0_s = _dot(t_b, _bf(_dot(_bf(d_ak), v_s)))
        w_b, u0_b = _bf(w_s), _bf(u0_s)
        d_rb_b = _bf(d_rb)
        y0_s = _dot(d_rb_b, u0_b) + _dot(_bf(d_rk), v_s)
        rp_s = stack(r_t[:, sl]) + _dot(d_rb_b, w_b)
        y0 = y0_s[:chunk] + y0_s[chunk:]
        rp = rp_s[:chunk] + rp_s[chunk:]
        b_es = _bf(stack(b_e[:, sl]))
        k_es = _bf(stack(k_e[:, sl]))
        p_mat = lane_eye * gam_l[:, sl] + _dot_tn(b_es, w_b)
        q_mat = _dot_tn(b_es, u0_b) + _dot_tn(k_es, v_s)
        h_old = h_ref[p]
        h_b = _bf(h_old)
        o_ref[:, sl] = _dot(_bf(rp), h_b) + y0
        h_ref[p] = _dot(_bf(p_mat), h_b) + q_mat


def _wkv7(r, lw, k, v, kk, ab, *, chunk):
    bsz, seq, dmix = r.shape
    npair = dmix // LANES
    tile = pl.BlockSpec((None, chunk, dmix), lambda b, c: (b, c, 0))
    return pl.pallas_call(
        functools.partial(_wkv7_body, chunk=chunk, npair=npair),
        grid=(bsz, seq // chunk),
        in_specs=[tile] * 6,
        out_specs=tile,
        out_shape=jax.ShapeDtypeStruct((bsz, seq, dmix), F32),
        scratch_shapes=[pltpu.VMEM((npair, LANES, LANES), F32)],
        compiler_params=_params("parallel", "arbitrary"),
        name="wkv7",
    )(r, lw, k, v, kk, ab)


def _rwkv_out_body(o_ref, g_ref, bonus_ref, qm_ref, za_ref, zb_ref, x_ref, kv_ref, wo_ref,
                   lg_ref, lb_ref, fg_ref, e_ref, out_ref, y_ref, *, dmix):
    half = za_ref.shape[1]
    inv_n = 1.0 / RWKV_HEAD_DIM
    for i in range(dmix // LANES):
        sl = slice(i * LANES, (i + 1) * LANES)
        seg = o_ref[:, sl]
        d = seg - _split2_dot(seg, e_ref[...]) * inv_n
        var = _split2_dot(d * d, e_ref[...]) * inv_n
        ymix = (d * lax.rsqrt(var + RWKV_GN_EPS) * lg_ref[:, sl] + lb_ref[:, sl] + bonus_ref[:, sl]) * g_ref[:, sl]
        z = za_ref[:, sl] if (i + 1) * LANES <= half else zb_ref[:, i * LANES - half:(i + 1) * LANES - half]
        y_ref[:, sl] = _bf(ymix * _silu(z))
    _memory_attention_into(y_ref, qm_ref[...], kv_ref, zb_ref[:, dmix - half:], dmix)
    xn = x_ref[...] + _dot(y_ref[...], wo_ref[...])
    ms = jnp.mean(xn * xn, axis=-1, keepdims=True)
    out_ref[...] = xn * lax.rsqrt(ms + RMS_EPS) * fg_ref[...]


def _rwkv_out(o, g, bonus, proj, x, kv, w_out, lnx_g, lnx_b, final_g, e2, *, tm, qm_block, z_block):
    bsz, seq, dmix = o.shape
    dmodel = x.shape[-1]
    dx = XATTN_HEADS * XATTN_HEAD_DIM
    dinner = dmix + dx
    half = dinner // 2
    nmem = kv.shape[1]
    full = lambda shape: pl.BlockSpec(shape, lambda b, s: (0,) * len(shape))
    tile = lambda w, j: pl.BlockSpec((None, tm, w), lambda b, s: (b, s, j))
    return pl.pallas_call(
        functools.partial(_rwkv_out_body, dmix=dmix),
        grid=(bsz, seq // tm),
        in_specs=[tile(dmix, 0), tile(dmix, 0), tile(dmix, 0),
                  tile(dx, qm_block), tile(half, z_block), tile(half, z_block + 1), tile(dmodel, 0),
                  pl.BlockSpec((None, nmem, 2 * dx), lambda b, s: (b, 0, 0)),
                  full(w_out.shape), full((1, dmix)), full((1, dmix)), full((1, dmodel)), full(e2.shape)],
        out_specs=tile(dmodel, 0),
        out_shape=jax.ShapeDtypeStruct((bsz, seq, dmodel), F32),
        scratch_shapes=[pltpu.VMEM((tm, dinner), BF16)],
        compiler_params=_params("parallel", "parallel"),
        name="rwkv_out",
    )(o, g, bonus, proj, proj, proj, x, kv, w_out, lnx_g.reshape(1, dmix), lnx_b.reshape(1, dmix),
      final_g.reshape(1, dmodel), e2)


def _blockdiag_tiles(w, tile):
    nb, blk, _ = w.shape
    per = tile // blk
    wt = w.reshape(nb // per, per, blk, blk)
    eye = jnp.eye(per, dtype=w.dtype)
    return jnp.einsum('tpio,pq->tpiqo', wt, eye).reshape(nb // per, tile, tile)


def _pad_rows(w, rows):
    return jnp.pad(w, ((0, rows - w.shape[0]), (0, 0)))


def _pad_cols(w, cols):
    return jnp.pad(w, ((0, 0), (0, cols - w.shape[1])))


def kernel(x, mem, norm_g, mem_norm_g, mem_kv_w, w_out, ml_w_in, ml_conv_w, ml_conv_b, ml_wq, ml_wk, ml_wv, ml_w_gate, ml_b_gate, ml_mhn_g, ml_skip, rw_w_in, rw_mu, rw_w_lora2, rw_w0, rw_a_lora2, rw_a0, rw_v_lora2, rw_v0, rw_g_lora2, rw_k_k, rw_k_a, rw_r_k, rw_lnx_g, rw_lnx_b, final_g):
    bsz, seq, dmodel = x.shape
    nmem = mem.shape[1]
    dmix = ml_conv_w.shape[-1]
    dx = XATTN_HEADS * XATTN_HEAD_DIM
    ntok = bsz * seq
    mem2d = mem.reshape(bsz * nmem, dmodel)

    kv0 = _rms_matmul(mem2d, mem_norm_g[0], _bf(mem_kv_w[0]), tm=512, tn=2 * dx,
                      out_dtype=BF16).reshape(bsz, nmem, 2 * dx)
    proj0 = _rms_matmul(x.reshape(ntok, dmodel), norm_g[0], _bf(ml_w_in[0]), tm=512, tn=1024,
                        out_dtype=F32).reshape(bsz, seq, -1)
    wqk = jnp.concatenate([_blockdiag_tiles(ml_wq[0], MXU_DIM), _blockdiag_tiles(ml_wk[0], MXU_DIM)], axis=-1)
    wv = _blockdiag_tiles(ml_wv[0], MXU_DIM)
    wg = ml_w_gate[0].reshape(3, dmix, -1)
    xc, q, k, v0, gcol, grow = _mlstm_pre(
        proj0, ml_conv_w[0], ml_conv_b[0], _bf(wqk), _bf(wv),
        _bf(jnp.pad(wg, ((0, 0), (0, 0), (0, LANES - wg.shape[-1])))),
        _bf(jnp.pad(wg.transpose(0, 2, 1), ((0, 0), (0, 16 - wg.shape[-1]), (0, 0)))),
        ml_b_gate[0], ts=256)
    hs = _mlstm_scan(q, k, v0, gcol, grow, chunk=MLSTM_CHUNK)
    x1 = _mlstm_out(hs, xc, proj0, x, kv0, _bf(w_out[0]), ml_mhn_g[0], ml_skip[0], tm=256)

    w_in = rw_w_in[0]
    mu = rw_mu[0]
    cuts = [0, 3 * dmix]
    for rank in (DECAY_RANK, ICLR_RANK, VRES_RANK, GATE_RANK):
        cuts.append(cuts[-1] + rank)
    lora_w = jnp.concatenate([_pad_cols(w_in[:, cuts[i]:cuts[i + 1]], LORA_SEG) for i in range(1, 5)], axis=1)
    lora_mu = jnp.concatenate([jnp.pad(mu[cuts[i]:cuts[i + 1]], (0, LORA_SEG - (cuts[i + 1] - cuts[i])))
                               for i in range(1, 5)])
    w_packed = jnp.concatenate([w_in[:, :3 * dmix], w_in[:, cuts[-1]:], lora_w], axis=1)
    qm_block = (3 * dmix) // dx
    z_block = (3 * dmix + dx) // ((dmix + dx) // 2)
    lora_block = (3 * dmix + dx + dmix + dx) // (4 * LORA_SEG)

    kv1 = _rms_matmul(mem2d, mem_norm_g[1], _bf(mem_kv_w[1]), tm=512, tn=2 * dx,
                      out_dtype=BF16).reshape(bsz, nmem, 2 * dx)
    proj1 = _rms_matmul(x1.reshape(ntok, dmodel), norm_g[1], _bf(w_packed), tm=512, tn=1536,
                        out_dtype=F32).reshape(bsz, seq, -1)
    wl2 = _bf(jnp.stack([_pad_rows(rw_w_lora2[0], LORA_SEG), _pad_rows(rw_a_lora2[0], LORA_SEG),
                         _pad_rows(rw_v_lora2[0], LORA_SEG), _pad_rows(rw_g_lora2[0], LORA_SEG)]))
    vecs = jnp.stack([rw_w0[0], rw_a0[0], rw_v0[0], rw_k_k[0], rw_k_a[0], rw_r_k[0].reshape(-1),
                      jnp.zeros_like(rw_w0[0]), jnp.zeros_like(rw_w0[0])])
    lane = jnp.arange(LANES) // RWKV_HEAD_DIM
    e2 = _bf(lane[:, None] == lane[None, :])
    mus = (mu[:dmix].reshape(1, -1), mu[dmix:2 * dmix].reshape(1, -1), mu[2 * dmix:3 * dmix].reshape(1, -1),
           lora_mu.reshape(1, -1))
    r, lw, k2, v1, kk, ab, g, bonus = _rwkv_pre(proj1, v0, mus, wl2, vecs, e2, tm=256, lora_block=lora_block)
    o = _wkv7(r, lw, k2, v1, kk, ab, chunk=WKV_CHUNK)
    return _rwkv_out(o, g, bonus, proj1, x1, kv1, _bf(w_out[1]), rw_lnx_g[0], rw_lnx_b[0], final_g, e2,
                     tm=256, qm_block=qm_block, z_block=z_block)
```

```python
import functools
import math

import jax
import jax.numpy as jnp
from jax import lax
from jax.experimental import pallas as pl
from jax.experimental.pallas import tpu as pltpu

F32 = jnp.float32
BF16 = jnp.bfloat16
HIGHEST = lax.Precision.HIGHEST

LANES = 128
MXU_DIM = 256
VMEM_LIMIT = 48 * 1024 * 1024

XATTN_HEADS = 4
XATTN_HEAD_DIM = 128
MLSTM_HEADS = 4
MLSTM_CONV = 4
QKV_BLOCK = 4
RWKV_HEAD_DIM = 64
DECAY_RANK, ICLR_RANK, VRES_RANK, GATE_RANK = 64, 64, 32, 128
LORA_SEG = 128

RMS_EPS = 1e-6
MHLN_EPS = 1e-5
RWKV_GN_EPS = 64e-5
L2_EPS = 1e-12

MLSTM_CHUNK = 256
WKV_CHUNK = 64
WKV_PAIR_GROUP = 6


def _params(*sem):
    return pltpu.CompilerParams(dimension_semantics=sem, vmem_limit_bytes=VMEM_LIMIT)


def _sigmoid(x):
    return 1.0 / (1.0 + jnp.exp(-x))


def _log_sigmoid(x):
    return jnp.minimum(x, 0.0) - jnp.log1p(jnp.exp(-jnp.abs(x)))


def _dot(a, b):
    return jnp.dot(a, b, preferred_element_type=F32)


def _dot_nt(a, b):
    return lax.dot_general(a, b, (((1,), (1,)), ((), ())), preferred_element_type=F32)


def _dot_tn(a, b):
    return lax.dot_general(a, b, (((0,), (0,)), ((), ())), preferred_element_type=F32)


def _bf(x):
    return x.astype(BF16)


def _f32(x):
    return x.astype(F32)


def _rms_matmul_body(x_ref, g_ref, w_ref, o_ref, h_ref):
    @pl.when(pl.program_id(1) == 0)
    def _():
        x = x_ref[...]
        ms = jnp.mean(x * x, axis=-1, keepdims=True)
        h_ref[...] = _bf(x * lax.rsqrt(ms + RMS_EPS) * g_ref[...])

    o_ref[...] = _dot(h_ref[...], w_ref[...]).astype(o_ref.dtype)


def _rms_matmul(x2d, g, w, *, tm, tn, out_dtype):
    m, d = x2d.shape
    n = w.shape[1]
    return pl.pallas_call(
        _rms_matmul_body,
        grid=(m // tm, n // tn),
        in_specs=[pl.BlockSpec((tm, d), lambda i, j: (i, 0)),
                  pl.BlockSpec((1, d), lambda i, j: (0, 0)),
                  pl.BlockSpec((d, tn), lambda i, j: (0, j))],
        out_specs=pl.BlockSpec((tm, tn), lambda i, j: (i, j)),
        out_shape=jax.ShapeDtypeStruct((m, n), out_dtype),
        scratch_shapes=[pltpu.VMEM((tm, d), BF16)],
        compiler_params=_params("parallel", "arbitrary"),
        name="rms_matmul",
    )(x2d, g.reshape(1, d), w)


def _mlstm_pre_body(u_ref, cw_ref, cb_ref, wqk_ref, wv_ref, wg_ref, wgt_ref, bgr_ref, bgc_ref,
                    xc_ref, q_ref, k_ref, v_ref, gcol_ref, grow_ref, ext_ref, *, ts, nblk):
    s = pl.program_id(1)

    @pl.when(s == 0)
    def _():
        ext_ref[0:8, :] = jnp.zeros((8, ext_ref.shape[1]), F32)

    @pl.when(s > 0)
    def _():
        ext_ref[0:8, :] = ext_ref[ts:ts + 8, :]

    ub = u_ref[...]
    u = _f32(ub)
    ext_ref[8:8 + ts, :] = u
    acc = cb_ref[...] + cw_ref[MLSTM_CONV - 1:MLSTM_CONV, :] * u
    for j in range(1, MLSTM_CONV):
        acc = acc + cw_ref[MLSTM_CONV - 1 - j:MLSTM_CONV - j, :] * ext_ref[8 - j:8 - j + ts, :]
    xc = acc * _sigmoid(acc)
    xcb = _bf(xc)
    xc_ref[...] = xcb

    for b in range(nblk):
        lo, hi = b * MXU_DIM, (b + 1) * MXU_DIM
        qk = _dot(xcb[:, lo:hi], wqk_ref[b])
        q_ref[:, lo:hi] = _bf(qk[:, :MXU_DIM])
        k_ref[:, lo:hi] = _bf(qk[:, MXU_DIM:])
        v_ref[:, lo:hi] = _bf(_dot(ub[:, lo:hi], wv_ref[b]))

    qb, kb, vb = q_ref[...], k_ref[...], v_ref[...]
    gc = _dot(qb, wg_ref[0]) + _dot(kb, wg_ref[1]) + _dot(vb, wg_ref[2]) + bgr_ref[...]
    lane = lax.broadcasted_iota(jnp.int32, gc.shape, 1)
    gc = jnp.where(lane < MLSTM_HEADS, gc, _log_sigmoid(gc))
    gcol_ref[...] = gc[:, :2 * MLSTM_HEADS]

    gr = _dot_nt(wgt_ref[0], qb) + _dot_nt(wgt_ref[1], kb) + _dot_nt(wgt_ref[2], vb)
    gr = gr[:2 * MLSTM_HEADS, :] + bgc_ref[...]
    row = lax.broadcasted_iota(jnp.int32, gr.shape, 0)
    grow_ref[...] = jnp.where(row < MLSTM_HEADS, gr, _log_sigmoid(gr))


def _mlstm_pre(proj, conv_w, conv_b, wqk, wv, wg, wgt, bg, *, ts):
    bsz, seq, _ = proj.shape
    dmix = conv_w.shape[1]
    nblk = dmix // MXU_DIM
    ng = 2 * MLSTM_HEADS
    full = lambda shape: pl.BlockSpec(shape, lambda b, s: (0,) * len(shape))
    tile = pl.BlockSpec((None, ts, dmix), lambda b, s: (b, s, 0))
    act = jax.ShapeDtypeStruct((bsz, seq, dmix), BF16)
    return pl.pallas_call(
        functools.partial(_mlstm_pre_body, ts=ts, nblk=nblk),
        grid=(bsz, seq // ts),
        in_specs=[tile, full(conv_w.shape), full((1, dmix)), full(wqk.shape), full(wv.shape),
                  full(wg.shape), full(wgt.shape), full((1, LANES)), full((ng, 1))],
        out_specs=[tile, tile, tile, tile,
                   pl.BlockSpec((None, ts, ng), lambda b, s: (b, s, 0)),
                   pl.BlockSpec((None, ng, ts), lambda b, s: (b, 0, s))],
        out_shape=[act, act, act, act,
                   jax.ShapeDtypeStruct((bsz, seq, ng), F32),
                   jax.ShapeDtypeStruct((bsz, ng, seq), F32)],
        scratch_shapes=[pltpu.VMEM((ts + 8, dmix), F32)],
        compiler_params=_params("parallel", "arbitrary"),
        name="mlstm_pre",
    )(proj, conv_w, conv_b.reshape(1, dmix), wqk, wv, wg, wgt,
      jnp.pad(bg, (0, LANES - ng)).reshape(1, LANES), bg.reshape(ng, 1))


def _mlstm_scan_body(q_ref, k_ref, v_ref, gcol_ref, grow_ref, h_ref, c_ref, n_ref, m_ref, *, chunk, dh):
    @pl.when(pl.program_id(1) == 0)
    def _():
        c_ref[...] = jnp.zeros(c_ref.shape, F32)
        n_ref[...] = jnp.zeros(n_ref.shape, F32)
        m_ref[...] = jnp.zeros(m_ref.shape, F32)

    gc = gcol_ref[...]
    gr = grow_ref[...]
    row = lax.broadcasted_iota(jnp.int32, (chunk, chunk), 0)
    col = lax.broadcasted_iota(jnp.int32, (chunk, chunk), 1)
    causal = col <= row
    tril = causal.astype(F32)
    bcol_all = jnp.dot(tril, gc, precision=HIGHEST, preferred_element_type=F32)
    brow_all = lax.dot_general(gr, tril, (((1,), (1,)), ((), ())), precision=HIGHEST,
                               preferred_element_type=F32)
    scale = dh ** -0.5
    for h in range(MLSTM_HEADS):
        lo, hi = h * dh, (h + 1) * dh
        i_col = gc[:, h:h + 1]
        b_col = bcol_all[:, MLSTM_HEADS + h:MLSTM_HEADS + h + 1]
        i_row = gr[h:h + 1, :]
        b_row = brow_all[MLSTM_HEADS + h:MLSTM_HEADS + h + 1, :]
        m_old = m_ref[h][0:1, 0:1]
        d_mat = jnp.where(causal, b_col - b_row + i_row, -jnp.inf)
        inter = b_col + m_old
        m_t = jnp.maximum(jnp.max(d_mat, axis=-1, keepdims=True), inter)
        qb = q_ref[:, lo:hi]
        qh = _f32(qb)
        kh = _f32(k_ref[:, lo:hi]) * scale
        kb, vb = _bf(kh), v_ref[:, lo:hi]
        s = _dot_nt(qb, kb) * jnp.exp(d_mat - m_t)
        g_in = jnp.exp(inter - m_t)
        c_old = c_ref[h]
        n_old = n_ref[h]
        num = _dot(_bf(s), vb) + g_in * _dot(qb, _bf(c_old))
        den = jnp.sum(s, axis=-1, keepdims=True) + g_in * jnp.sum(qh * n_old, axis=-1, keepdims=True)
        h_ref[:, lo:hi] = _bf(num / jnp.maximum(jnp.abs(den), jnp.exp(-m_t)))
        b_last = b_col[chunk - 1:chunk, :]
        w_s = b_last - b_col + i_col
        m_new = jnp.maximum(b_last + m_old, jnp.max(w_s, axis=0, keepdims=True))
        kw = kh * jnp.exp(w_s - m_new)
        g_old = jnp.exp(b_last + m_old - m_new)
        c_ref[h] = g_old * c_old + _dot_tn(_bf(kw), vb)
        n_ref[h] = g_old * n_old + jnp.sum(kw, axis=0, keepdims=True)
        m_ref[h] = jnp.broadcast_to(m_new, m_ref.shape[1:])


def _mlstm_scan(q, k, v, gcol, grow, *, chunk):
    bsz, seq, dmix = q.shape
    dh = dmix // MLSTM_HEADS
    ng = 2 * MLSTM_HEADS
    tile = pl.BlockSpec((None, chunk, dmix), lambda b, c: (b, c, 0))
    return pl.pallas_call(
        functools.partial(_mlstm_scan_body, chunk=chunk, dh=dh),
        grid=(bsz, seq // chunk),
        in_specs=[tile, tile, tile,
                  pl.BlockSpec((None, chunk, ng), lambda b, c: (b, c, 0)),
                  pl.BlockSpec((None, ng, chunk), lambda b, c: (b, 0, c))],
        out_specs=tile,
        out_shape=jax.ShapeDtypeStruct((bsz, seq, dmix), BF16),
        scratch_shapes=[pltpu.VMEM((MLSTM_HEADS, dh, dh), F32),
                        pltpu.VMEM((MLSTM_HEADS, 1, dh), F32),
                        pltpu.VMEM((MLSTM_HEADS, 8, LANES), F32)],
        compiler_params=_params("parallel", "arbitrary"),
        name="mlstm_scan",
    )(q, k, v, gcol, grow)


def _silu(z):
    return z * _sigmoid(z)


def _memory_attention_into(y_ref, qm, kv_ref, z_attn, col0):
    dx = XATTN_HEADS * XATTN_HEAD_DIM
    for h in range(XATTN_HEADS):
        lo, hi = h * XATTN_HEAD_DIM, (h + 1) * XATTN_HEAD_DIM
        s = _dot_nt(qm[:, lo:hi], kv_ref[:, lo:hi]) * (XATTN_HEAD_DIM ** -0.5)
        p = jnp.exp(s - jnp.max(s, axis=-1, keepdims=True))
        o = _dot(_bf(p), kv_ref[:, dx + lo:dx + hi]) / jnp.sum(p, axis=-1, keepdims=True)
        y_ref[:, col0 + lo:col0 + hi] = _bf(o * _silu(_f32(z_attn[:, lo:hi])))


def _mlstm_out_body(hs_ref, xc_ref, qm_ref, z_ref, x_ref, kv_ref, wo_ref, mg_ref, sk_ref,
                    o_ref, y_ref, *, dh):
    dmix = MLSTM_HEADS * dh
    for h in range(MLSTM_HEADS):
        lo, hi = h * dh, (h + 1) * dh
        seg = _f32(hs_ref[:, lo:hi])
        d = seg - jnp.mean(seg, axis=-1, keepdims=True)
        var = jnp.mean(d * d, axis=-1, keepdims=True)
        ymix = d * lax.rsqrt(var + MHLN_EPS) * mg_ref[:, lo:hi] + sk_ref[:, lo:hi] * _f32(xc_ref[:, lo:hi])
        y_ref[:, lo:hi] = _bf(ymix * _silu(_f32(z_ref[:, lo:hi])))
    _memory_attention_into(y_ref, qm_ref[...], kv_ref, z_ref[:, dmix:], dmix)
    o_ref[...] = x_ref[...] + _dot(y_ref[...], wo_ref[...])


def _mlstm_out(hs, xc, proj, x, kv, w_out, mhn_g, skip, *, tm):
    bsz, seq, dmix = hs.shape
    dmodel = x.shape[-1]
    dx = XATTN_HEADS * XATTN_HEAD_DIM
    dinner = dmix + dx
    nmem = kv.shape[1]
    full = lambda shape: pl.BlockSpec(shape, lambda b, s: (0,) * len(shape))
    tile = lambda w, j: pl.BlockSpec((None, tm, w), lambda b, s: (b, s, j))
    return pl.pallas_call(
        functools.partial(_mlstm_out_body, dh=dmix // MLSTM_HEADS),
        grid=(bsz, seq // tm),
        in_specs=[tile(dmix, 0), tile(dmix, 0),
                  tile(dx, dmix // dx), tile(dinner, 1), tile(dmodel, 0),
                  pl.BlockSpec((None, nmem, 2 * dx), lambda b, s: (b, 0, 0)),
                  full(w_out.shape), full((1, dmix)), full((1, dmix))],
        out_specs=tile(dmodel, 0),
        out_shape=jax.ShapeDtypeStruct((bsz, seq, dmodel), F32),
        scratch_shapes=[pltpu.VMEM((tm, dinner), BF16)],
        compiler_params=_params("parallel", "parallel"),
        name="mlstm_out",
    )(hs, xc, proj, proj, x, kv, w_out, mhn_g.reshape(1, dmix), skip.reshape(1, dmix))


def _split2_dot(x, e):
    hi = _bf(x)
    lo = _bf(x - hi.astype(F32))
    return _dot(hi, e) + _dot(lo, e)


def _head_sums(x, e_ref):
    nslab = x.shape[1] // LANES
    return jnp.concatenate([_split2_dot(x[:, i * LANES:(i + 1) * LANES], e_ref[...]) for i in range(nslab)],
                           axis=1)


def _token_shift(x, carry_ref, mu):
    tm = x.shape[0]
    prev = pltpu.roll(x, 1, axis=0)
    first = lax.broadcasted_iota(jnp.int32, (tm, 1), 0) == 0
    prev = jnp.where(first, carry_ref[7:8, :], prev)
    carry_ref[...] = x[tm - 8:tm, :]
    return x + (prev - x) * mu


def _rwkv_pre_body(pr_ref, pk_ref, pv_ref, pl_ref, vf_ref, mur_ref, muk_ref, muv_ref, mul_ref,
                   wl2_ref, vecs_ref, e_ref,
                   r_ref, lw_ref, k_ref, v_ref, kk_ref, ab_ref, g_ref, bonus_ref,
                   cr_ref, ck_ref, cv_ref, cl_ref):
    @pl.when(pl.program_id(1) == 0)
    def _():
        for c in (cr_ref, ck_ref, cv_ref, cl_ref):
            c[...] = jnp.zeros(c.shape, F32)

    r = _token_shift(_f32(pr_ref[...]), cr_ref, mur_ref[...])
    k = _token_shift(_f32(pk_ref[...]), ck_ref, muk_ref[...])
    v = _token_shift(_f32(pv_ref[...]), cv_ref, muv_ref[...])
    lo = _token_shift(_f32(pl_ref[...]), cl_ref, mul_ref[...])
    w0, a0, v0, k_k, k_a, r_k = (vecs_ref[i:i + 1, :] for i in range(6))

    seg = lambda i: lo[:, i * LORA_SEG:(i + 1) * LORA_SEG]
    d = w0 + _dot(_bf(jnp.tanh(seg(0))), wl2_ref[0])
    lw_ref[...] = -math.exp(-0.5) * _sigmoid(d)
    a = _sigmoid(a0 + _dot(_bf(seg(1)), wl2_ref[1]))
    v = v + (_f32(vf_ref[...]) - v) * _sigmoid(v0 + _dot(_bf(seg(2)), wl2_ref[2]))
    g_ref[...] = _bf(_dot(_bf(_sigmoid(seg(3))), wl2_ref[3]))

    kk = k * k_k
    kk = kk / jnp.maximum(jnp.sqrt(_head_sums(kk * kk, e_ref)), L2_EPS)
    k2 = k * (1.0 + (a - 1.0) * k_a)
    r_ref[...] = _bf(r)
    k_ref[...] = _bf(k2)
    v_ref[...] = _bf(v)
    kk_ref[...] = _bf(kk)
    ab_ref[...] = _bf(kk * a)
    bonus_ref[...] = _bf(_head_sums(r * k2 * r_k, e_ref) * v)


def _rwkv_pre(proj, v_first, mus, wl2, vecs, e2, *, tm, lora_block):
    bsz, seq, dmix = v_first.shape
    lw = 4 * LORA_SEG
    full = lambda shape: pl.BlockSpec(shape, lambda b, s: (0,) * len(shape))
    tile = lambda w, j: pl.BlockSpec((None, tm, w), lambda b, s: (b, s, j))
    act = jax.ShapeDtypeStruct((bsz, seq, dmix), BF16)
    log_decay = jax.ShapeDtypeStruct((bsz, seq, dmix), F32)
    return pl.pallas_call(
        _rwkv_pre_body,
        grid=(bsz, seq // tm),
        in_specs=[tile(dmix, 0), tile(dmix, 1), tile(dmix, 2), tile(lw, lora_block), tile(dmix, 0),
                  full((1, dmix)), full((1, dmix)), full((1, dmix)), full((1, lw)),
                  full(wl2.shape), full(vecs.shape), full(e2.shape)],
        out_specs=[tile(dmix, 0)] * 8,
        out_shape=[act, log_decay] + [act] * 6,
        scratch_shapes=[pltpu.VMEM((8, dmix), F32)] * 3 + [pltpu.VMEM((8, lw), F32)],
        compiler_params=_params("parallel", "arbitrary"),
        name="rwkv_pre",
    )(proj, proj, proj, proj, v_first, *mus, wl2, vecs, e2)


def _wkv7_body(r_ref, lw_ref, k_ref, v_ref, kk_ref, ab_ref, o_ref, h_ref, *, chunk, npair, group):
    @pl.when(pl.program_id(1) == 0)
    def _():
        h_ref[...] = jnp.zeros(h_ref.shape, F32)

    c2 = 2 * chunk
    row = lax.broadcasted_iota(jnp.int32, (chunk, chunk), 0)
    col = lax.broadcasted_iota(jnp.int32, (chunk, chunk), 1)
    tril = (col <= row).astype(F32)
    lw = lw_ref[...]
    g_inc = jnp.dot(tril, lw, precision=HIGHEST, preferred_element_type=F32)
    g_exc = g_inc - lw
    g_last = g_inc[chunk - 1:chunk, :]
    e_inc = jnp.exp(g_inc)
    e_neg = jnp.exp(-g_inc)
    e_end = jnp.exp(g_last - g_inc)
    kk = _f32(kk_ref[...])
    ab = _f32(ab_ref[...])
    kx = _f32(k_ref[...])
    a_t = -kk * jnp.exp(g_exc)
    b_t = ab * e_neg
    k_t = kx * e_neg
    r_t = _f32(r_ref[...]) * e_inc
    b_e = ab * e_end
    k_e = kx * e_end
    gam_l = jnp.exp(g_last)

    lane_head = lax.broadcasted_iota(jnp.int32, (c2, LANES), 1) // RWKV_HEAD_DIM
    row_head = lax.broadcasted_iota(jnp.int32, (c2, LANES), 0) // chunk
    head_mask = lane_head == row_head
    ri = lax.broadcasted_iota(jnp.int32, (c2, c2), 0)
    ci = lax.broadcasted_iota(jnp.int32, (c2, c2), 1)
    same = (ri // chunk) == (ci // chunk)
    strict = same & (ci < ri)
    incl = same & (ci <= ri)
    eye = (ri == ci).astype(F32)
    lane_eye = (lax.broadcasted_iota(jnp.int32, (LANES, LANES), 0)
                == lax.broadcasted_iota(jnp.int32, (LANES, LANES), 1)).astype(F32)

    def stack(x):
        return jnp.where(head_mask, jnp.concatenate([x, x], axis=0), 0.0)

    def dup(x):
        return jnp.concatenate([x, x], axis=0)

    each = lambda f, *xs: [f(*t) for t in zip(*xs)]
    for g0 in range(0, npair, group):
        sls = [slice(p * LANES, (p + 1) * LANES) for p in range(g0, g0 + group)]
        a_s = [_bf(stack(a_t[:, sl])) for sl in sls]
        r_sf = [stack(r_t[:, sl]) for sl in sls]
        r_s = each(_bf, r_sf)
        b_d = [_bf(dup(b_t[:, sl])) for sl in sls]
        k_d = [_bf(dup(k_t[:, sl])) for sl in sls]
        v_s = [stack(v_ref[:, sl]) for sl in sls]
        d_ab = each(lambda x, y: jnp.where(strict, _dot_nt(x, y), 0.0), a_s, b_d)
        d_ak = each(lambda x, y: _bf(jnp.where(strict, _dot_nt(x, y), 0.0)), a_s, k_d)
        d_rb = each(lambda x, y: _bf(jnp.where(incl, _dot_nt(x, y), 0.0)), r_s, b_d)
        d_rk = each(lambda x, y: _bf(jnp.where(incl, _dot_nt(x, y), 0.0)), r_s, k_d)
        t_inv = [eye + d for d in d_ab]
        d_pow = each(_bf, d_ab)
        for _ in range(int(math.log2(chunk)) - 1):
            d_pow = each(lambda x: _bf(_dot(x, x)), d_pow)
            t_inv = each(lambda t, dp: t + _dot(dp, _bf(t)), t_inv, d_pow)
        t_b = each(_bf, t_inv)
        akv = each(lambda x, y: _bf(_dot(x, y)), d_ak, v_s)
        w_b = each(lambda t, x: _bf(_dot(t, x)), t_b, a_s)
        u0_b = each(lambda t, x: _bf(_dot(t, x)), t_b, akv)
        y0_s = each(lambda drb, u0, drk, v: _dot(drb, u0) + _dot(drk, v), d_rb, u0_b, d_rk, v_s)
        rp_s = each(lambda r, drb, w: r + _dot(drb, w), r_sf, d_rb, w_b)
        b_es = [_bf(stack(b_e[:, sl])) for sl in sls]
        k_es = [_bf(stack(k_e[:, sl])) for sl in sls]
        p_mat = [lane_eye * gam_l[:, sl] + _dot_tn(be, w) for sl, be, w in zip(sls, b_es, w_b)]
        q_mat = each(lambda be, u0, ke, v: _dot_tn(be, u0) + _dot_tn(ke, v), b_es, u0_b, k_es, v_s)
        for i, sl in enumerate(sls):
            h_b = _bf(h_ref[g0 + i])
            rp = rp_s[i][:chunk] + rp_s[i][chunk:]
            o_ref[:, sl] = _bf(_dot(_bf(rp), h_b) + (y0_s[i][:chunk] + y0_s[i][chunk:]))
            h_ref[g0 + i] = _dot(_bf(p_mat[i]), h_b) + q_mat[i]


def _wkv7(r, lw, k, v, kk, ab, *, chunk):
    bsz, seq, dmix = r.shape
    npair = dmix // LANES
    tile = pl.BlockSpec((None, chunk, dmix), lambda b, c: (b, c, 0))
    return pl.pallas_call(
        functools.partial(_wkv7_body, chunk=chunk, npair=npair, group=WKV_PAIR_GROUP),
        grid=(bsz, seq // chunk),
        in_specs=[tile] * 6,
        out_specs=tile,
        out_shape=jax.ShapeDtypeStruct((bsz, seq, dmix), BF16),
        scratch_shapes=[pltpu.VMEM((npair, LANES, LANES), F32)],
        compiler_params=_params("parallel", "arbitrary"),
        name="wkv7",
    )(r, lw, k, v, kk, ab)


def _rwkv_out_body(o_ref, g_ref, bonus_ref, qm_ref, za_ref, zb_ref, x_ref, kv_ref, wo_ref,
                   lg_ref, lb_ref, fg_ref, e_ref, out_ref, y_ref, *, dmix):
    half = za_ref.shape[1]
    inv_n = 1.0 / RWKV_HEAD_DIM
    for i in range(dmix // LANES):
        sl = slice(i * LANES, (i + 1) * LANES)
        seg = _f32(o_ref[:, sl])
        d = seg - _split2_dot(seg, e_ref[...]) * inv_n
        var = _split2_dot(d * d, e_ref[...]) * inv_n
        ymix = ((d * lax.rsqrt(var + RWKV_GN_EPS) * lg_ref[:, sl] + lb_ref[:, sl] + _f32(bonus_ref[:, sl]))
                * _f32(g_ref[:, sl]))
        z = za_ref[:, sl] if (i + 1) * LANES <= half else zb_ref[:, i * LANES - half:(i + 1) * LANES - half]
        y_ref[:, sl] = _bf(ymix * _silu(_f32(z)))
    _memory_attention_into(y_ref, qm_ref[...], kv_ref, zb_ref[:, dmix - half:], dmix)
    xn = x_ref[...] + _dot(y_ref[...], wo_ref[...])
    ms = jnp.mean(xn * xn, axis=-1, keepdims=True)
    out_ref[...] = xn * lax.rsqrt(ms + RMS_EPS) * fg_ref[...]


def _rwkv_out(o, g, bonus, proj, x, kv, w_out, lnx_g, lnx_b, final_g, e2, *, tm, qm_block, z_block):
    bsz, seq, dmix = o.shape
    dmodel = x.shape[-1]
    dx = XATTN_HEADS * XATTN_HEAD_DIM
    dinner = dmix + dx
    half = dinner // 2
    nmem = kv.shape[1]
    full = lambda shape: pl.BlockSpec(shape, lambda b, s: (0,) * len(shape))
    tile = lambda w, j: pl.BlockSpec((None, tm, w), lambda b, s: (b, s, j))
    return pl.pallas_call(
        functools.partial(_rwkv_out_body, dmix=dmix),
        grid=(bsz, seq // tm),
        in_specs=[tile(dmix, 0), tile(dmix, 0), tile(dmix, 0),
                  tile(dx, qm_block), tile(half, z_block), tile(half, z_block + 1), tile(dmodel, 0),
                  pl.BlockSpec((None, nmem, 2 * dx), lambda b, s: (b, 0, 0)),
                  full(w_out.shape), full((1, dmix)), full((1, dmix)), full((1, dmodel)), full(e2.shape)],
        out_specs=tile(dmodel, 0),
        out_shape=jax.ShapeDtypeStruct((bsz, seq, dmodel), F32),
        scratch_shapes=[pltpu.VMEM((tm, dinner), BF16)],
        compiler_params=_params("parallel", "parallel"),
        name="rwkv_out",
    )(o, g, bonus, proj, proj, proj, x, kv, w_out, lnx_g.reshape(1, dmix), lnx_b.reshape(1, dmix),
      final_g.reshape(1, dmodel), e2)


def _blockdiag_tiles(w, tile):
    nb, blk, _ = w.shape
    rows = w.reshape(nb * blk // tile, tile, blk)
    idx = jnp.arange(tile)
    on_diag = (idx[:, None] // blk) == (idx[None, :] // blk)
    return jnp.where(on_diag, jnp.tile(rows, (1, 1, tile // blk)), 0.0)


def _pad_rows(w, rows):
    return jnp.pad(w, ((0, rows - w.shape[0]), (0, 0)))


def _pad_cols(w, cols):
    return jnp.pad(w, ((0, 0), (0, cols - w.shape[1])))


def kernel(x, mem, norm_g, mem_norm_g, mem_kv_w, w_out, ml_w_in, ml_conv_w, ml_conv_b, ml_wq, ml_wk, ml_wv, ml_w_gate, ml_b_gate, ml_mhn_g, ml_skip, rw_w_in, rw_mu, rw_w_lora2, rw_w0, rw_a_lora2, rw_a0, rw_v_lora2, rw_v0, rw_g_lora2, rw_k_k, rw_k_a, rw_r_k, rw_lnx_g, rw_lnx_b, final_g):
    bsz, seq, dmodel = x.shape
    nmem = mem.shape[1]
    dmix = ml_conv_w.shape[-1]
    dx = XATTN_HEADS * XATTN_HEAD_DIM
    ntok = bsz * seq
    mem2d = mem.reshape(bsz * nmem, dmodel)

    kv0 = _rms_matmul(mem2d, mem_norm_g[0], _bf(mem_kv_w[0]), tm=512, tn=2 * dx,
                      out_dtype=BF16).reshape(bsz, nmem, 2 * dx)
    proj0 = _rms_matmul(x.reshape(ntok, dmodel), norm_g[0], _bf(ml_w_in[0]), tm=512, tn=1024,
                        out_dtype=BF16).reshape(bsz, seq, -1)
    wqk = jnp.concatenate([_blockdiag_tiles(ml_wq[0], MXU_DIM), _blockdiag_tiles(ml_wk[0], MXU_DIM)], axis=-1)
    wv = _blockdiag_tiles(ml_wv[0], MXU_DIM)
    wg = ml_w_gate[0].reshape(3, dmix, -1)
    xc, q, k, v0, gcol, grow = _mlstm_pre(
        proj0, ml_conv_w[0], ml_conv_b[0], _bf(wqk), _bf(wv),
        _bf(jnp.pad(wg, ((0, 0), (0, 0), (0, LANES - wg.shape[-1])))),
        _bf(jnp.pad(wg.transpose(0, 2, 1), ((0, 0), (0, 16 - wg.shape[-1]), (0, 0)))),
        ml_b_gate[0], ts=256)
    hs = _mlstm_scan(q, k, v0, gcol, grow, chunk=MLSTM_CHUNK)
    x1 = _mlstm_out(hs, xc, proj0, x, kv0, _bf(w_out[0]), ml_mhn_g[0], ml_skip[0], tm=256)

    w_in = rw_w_in[0]
    mu = rw_mu[0]
    cuts = [0, 3 * dmix]
    for rank in (DECAY_RANK, ICLR_RANK, VRES_RANK, GATE_RANK):
        cuts.append(cuts[-1] + rank)
    lora_w = jnp.concatenate([_pad_cols(w_in[:, cuts[i]:cuts[i + 1]], LORA_SEG) for i in range(1, 5)], axis=1)
    lora_mu = jnp.concatenate([jnp.pad(mu[cuts[i]:cuts[i + 1]], (0, LORA_SEG - (cuts[i + 1] - cuts[i])))
                               for i in range(1, 5)])
    w_packed = jnp.concatenate([w_in[:, :3 * dmix], w_in[:, cuts[-1]:], lora_w], axis=1)
    qm_block = (3 * dmix) // dx
    z_block = (3 * dmix + dx) // ((dmix + dx) // 2)
    lora_block = (3 * dmix + dx + dmix + dx) // (4 * LORA_SEG)

    kv1 = _rms_matmul(mem2d, mem_norm_g[1], _bf(mem_kv_w[1]), tm=512, tn=2 * dx,
                      out_dtype=BF16).reshape(bsz, nmem, 2 * dx)
    proj1 = _rms_matmul(x1.reshape(ntok, dmodel), norm_g[1], _bf(w_packed), tm=512, tn=1536,
                        out_dtype=BF16).reshape(bsz, seq, -1)
    wl2 = _bf(jnp.stack([_pad_rows(rw_w_lora2[0], LORA_SEG), _pad_rows(rw_a_lora2[0], LORA_SEG),
                         _pad_rows(rw_v_lora2[0], LORA_SEG), _pad_rows(rw_g_lora2[0], LORA_SEG)]))
    vecs = jnp.stack([rw_w0[0], rw_a0[0], rw_v0[0], rw_k_k[0], rw_k_a[0], rw_r_k[0].reshape(-1),
                      jnp.zeros_like(rw_w0[0]), jnp.zeros_like(rw_w0[0])])
    lane = jnp.arange(LANES) // RWKV_HEAD_DIM
    e2 = _bf(lane[:, None] == lane[None, :])
    mus = (mu[:dmix].reshape(1, -1), mu[dmix:2 * dmix].reshape(1, -1), mu[2 * dmix:3 * dmix].reshape(1, -1),
           lora_mu.reshape(1, -1))
    r, lw, k2, v1, kk, ab, g, bonus = _rwkv_pre(proj1, v0, mus, wl2, vecs, e2, tm=256, lora_block=lora_block)
    o = _wkv7(r, lw, k2, v1, kk, ab, chunk=WKV_CHUNK)
    return _rwkv_out(o, g, bonus, proj1, x1, kv1, _bf(w_out[1]), rw_lnx_g[0], rw_lnx_b[0], final_g, e2,
                     tm=256, qm_block=qm_block, z_block=z_block)
```

```python
import functools
import math

import jax
import jax.numpy as jnp
from jax import lax
from jax.experimental import pallas as pl
from jax.experimental.pallas import tpu as pltpu

F32 = jnp.float32
BF16 = jnp.bfloat16
HIGHEST = lax.Precision.HIGHEST

LANES = 128
MXU_DIM = 256
VMEM_LIMIT = 48 * 1024 * 1024

XATTN_HEADS = 4
XATTN_HEAD_DIM = 128
MLSTM_HEADS = 4
MLSTM_CONV = 4
QKV_BLOCK = 4
RWKV_HEAD_DIM = 64
DECAY_RANK, ICLR_RANK, VRES_RANK, GATE_RANK = 64, 64, 32, 128
LORA_SEG = 128

RMS_EPS = 1e-6
MHLN_EPS = 1e-5
RWKV_GN_EPS = 64e-5
L2_EPS = 1e-12

MLSTM_CHUNK = 256
WKV_CHUNK = 64
WKV_PAIR_GROUP = 12


def _params(*sem):
    return pltpu.CompilerParams(dimension_semantics=sem, vmem_limit_bytes=VMEM_LIMIT)


def _sigmoid(x):
    return 0.5 * jnp.tanh(0.5 * x) + 0.5


def _log_sigmoid(x):
    return jnp.minimum(x, 0.0) - jnp.log1p(jnp.exp(-jnp.abs(x)))


def _dot(a, b):
    return jnp.dot(a, b, preferred_element_type=F32)


def _dot_nt(a, b):
    return lax.dot_general(a, b, (((1,), (1,)), ((), ())), preferred_element_type=F32)


def _dot_tn(a, b):
    return lax.dot_general(a, b, (((0,), (0,)), ((), ())), preferred_element_type=F32)


def _bf(x):
    return x.astype(BF16)


def _f32(x):
    return x.astype(F32)


def _rms_matmul_body(x_ref, g_ref, w_ref, o_ref, h_ref):
    @pl.when(pl.program_id(1) == 0)
    def _():
        x = x_ref[...]
        ms = jnp.mean(x * x, axis=-1, keepdims=True)
        h_ref[...] = _bf(x * lax.rsqrt(ms + RMS_EPS) * g_ref[...])

    o_ref[...] = _dot(h_ref[...], w_ref[...]).astype(o_ref.dtype)


def _rms_matmul(x2d, g, w, *, tm, tn, out_dtype):
    m, d = x2d.shape
    n = w.shape[1]
    return pl.pallas_call(
        _rms_matmul_body,
        grid=(m // tm, n // tn),
        in_specs=[pl.BlockSpec((tm, d), lambda i, j: (i, 0)),
                  pl.BlockSpec((1, d), lambda i, j: (0, 0)),
                  pl.BlockSpec((d, tn), lambda i, j: (0, j))],
        out_specs=pl.BlockSpec((tm, tn), lambda i, j: (i, j)),
        out_shape=jax.ShapeDtypeStruct((m, n), out_dtype),
        scratch_shapes=[pltpu.VMEM((tm, d), BF16)],
        compiler_params=_params("parallel", "arbitrary"),
        name="rms_matmul",
    )(x2d, g.reshape(1, d), w)


def _mlstm_pre_body(u_ref, cw_ref, cb_ref, wqk_ref, wv_ref, wg_ref, wgt_ref, bgr_ref, bgc_ref,
                    xc_ref, q_ref, k_ref, v_ref, gcol_ref, grow_ref, ext_ref, *, ts, nblk):
    s = pl.program_id(1)

    @pl.when(s == 0)
    def _():
        ext_ref[0:8, :] = jnp.zeros((8, ext_ref.shape[1]), F32)

    @pl.when(s > 0)
    def _():
        ext_ref[0:8, :] = ext_ref[ts:ts + 8, :]

    ub = u_ref[...]
    u = _f32(ub)
    ext_ref[8:8 + ts, :] = u
    acc = cb_ref[...] + cw_ref[MLSTM_CONV - 1:MLSTM_CONV, :] * u
    for j in range(1, MLSTM_CONV):
        acc = acc + cw_ref[MLSTM_CONV - 1 - j:MLSTM_CONV - j, :] * ext_ref[8 - j:8 - j + ts, :]
    xc = acc * _sigmoid(acc)
    xcb = _bf(xc)
    xc_ref[...] = xcb

    for b in range(nblk):
        lo, hi = b * MXU_DIM, (b + 1) * MXU_DIM
        qk = _dot(xcb[:, lo:hi], wqk_ref[b])
        q_ref[:, lo:hi] = _bf(qk[:, :MXU_DIM])
        k_ref[:, lo:hi] = _bf(qk[:, MXU_DIM:])
        v_ref[:, lo:hi] = _bf(_dot(ub[:, lo:hi], wv_ref[b]))

    qb, kb, vb = q_ref[...], k_ref[...], v_ref[...]
    gc = _dot(qb, wg_ref[0]) + _dot(kb, wg_ref[1]) + _dot(vb, wg_ref[2]) + bgr_ref[...]
    lane = lax.broadcasted_iota(jnp.int32, gc.shape, 1)
    gc = jnp.where(lane < MLSTM_HEADS, gc, _log_sigmoid(gc))
    gcol_ref[...] = gc[:, :2 * MLSTM_HEADS]

    gr = _dot_nt(wgt_ref[0], qb) + _dot_nt(wgt_ref[1], kb) + _dot_nt(wgt_ref[2], vb)
    gr = gr[:2 * MLSTM_HEADS, :] + bgc_ref[...]
    row = lax.broadcasted_iota(jnp.int32, gr.shape, 0)
    grow_ref[...] = jnp.where(row < MLSTM_HEADS, gr, _log_sigmoid(gr))


def _mlstm_pre(proj, conv_w, conv_b, wqk, wv, wg, wgt, bg, *, ts):
    bsz, seq, _ = proj.shape
    dmix = conv_w.shape[1]
    nblk = dmix // MXU_DIM
    ng = 2 * MLSTM_HEADS
    full = lambda shape: pl.BlockSpec(shape, lambda b, s: (0,) * len(shape))
    tile = pl.BlockSpec((None, ts, dmix), lambda b, s: (b, s, 0))
    act = jax.ShapeDtypeStruct((bsz, seq, dmix), BF16)
    return pl.pallas_call(
        functools.partial(_mlstm_pre_body, ts=ts, nblk=nblk),
        grid=(bsz, seq // ts),
        in_specs=[tile, full(conv_w.shape), full((1, dmix)), full(wqk.shape), full(wv.shape),
                  full(wg.shape), full(wgt.shape), full((1, LANES)), full((ng, 1))],
        out_specs=[tile, tile, tile, tile,
                   pl.BlockSpec((None, ts, ng), lambda b, s: (b, s, 0)),
                   pl.BlockSpec((None, ng, ts), lambda b, s: (b, 0, s))],
        out_shape=[act, act, act, act,
                   jax.ShapeDtypeStruct((bsz, seq, ng), F32),
                   jax.ShapeDtypeStruct((bsz, ng, seq), F32)],
        scratch_shapes=[pltpu.VMEM((ts + 8, dmix), F32)],
        compiler_params=_params("parallel", "arbitrary"),
        name="mlstm_pre",
    )(proj, conv_w, conv_b.reshape(1, dmix), wqk, wv, wg, wgt,
      jnp.pad(bg, (0, LANES - ng)).reshape(1, LANES), bg.reshape(ng, 1))


def _mlstm_scan_body(q_ref, k_ref, v_ref, gcol_ref, grow_ref, h_ref, c_ref, n_ref, m_ref, *, chunk, dh):
    @pl.when(pl.program_id(1) == 0)
    def _():
        c_ref[...] = jnp.zeros(c_ref.shape, F32)
        n_ref[...] = jnp.zeros(n_ref.shape, F32)
        m_ref[...] = jnp.zeros(m_ref.shape, F32)

    gc = gcol_ref[...]
    gr = grow_ref[...]
    row = lax.broadcasted_iota(jnp.int32, (chunk, chunk), 0)
    col = lax.broadcasted_iota(jnp.int32, (chunk, chunk), 1)
    causal = col <= row
    tril = causal.astype(F32)
    bcol_all = jnp.dot(tril, gc, precision=HIGHEST, preferred_element_type=F32)
    brow_all = lax.dot_general(gr, tril, (((1,), (1,)), ((), ())), precision=HIGHEST,
                               preferred_element_type=F32)
    scale = dh ** -0.5
    for h in range(MLSTM_HEADS):
        lo, hi = h * dh, (h + 1) * dh
        i_col = gc[:, h:h + 1]
        b_col = bcol_all[:, MLSTM_HEADS + h:MLSTM_HEADS + h + 1]
        i_row = gr[h:h + 1, :]
        b_row = brow_all[MLSTM_HEADS + h:MLSTM_HEADS + h + 1, :]
        m_old = m_ref[h][0:1, 0:1]
        d_mat = jnp.where(causal, b_col - b_row + i_row, -jnp.inf)
        inter = b_col + m_old
        m_t = jnp.maximum(jnp.max(d_mat, axis=-1, keepdims=True), inter)
        qb = q_ref[:, lo:hi]
        qh = _f32(qb)
        kh = _f32(k_ref[:, lo:hi]) * scale
        kb, vb = _bf(kh), v_ref[:, lo:hi]
        s = _dot_nt(qb, kb) * jnp.exp(d_mat - m_t)
        g_in = jnp.exp(inter - m_t)
        c_old = c_ref[h]
        n_old = n_ref[h]
        num = _dot(_bf(s), vb) + g_in * _dot(qb, _bf(c_old))
        den = jnp.sum(s, axis=-1, keepdims=True) + g_in * jnp.sum(qh * n_old, axis=-1, keepdims=True)
        h_ref[:, lo:hi] = _bf(num / jnp.maximum(jnp.abs(den), jnp.exp(-m_t)))
        b_last = b_col[chunk - 1:chunk, :]
        w_s = b_last - b_col + i_col
        m_new = jnp.maximum(b_last + m_old, jnp.max(w_s, axis=0, keepdims=True))
        kw = kh * jnp.exp(w_s - m_new)
        g_old = jnp.exp(b_last + m_old - m_new)
        c_ref[h] = g_old * c_old + _dot_tn(_bf(kw), vb)
        n_ref[h] = g_old * n_old + jnp.sum(kw, axis=0, keepdims=True)
        m_ref[h] = jnp.broadcast_to(m_new, m_ref.shape[1:])


def _mlstm_scan(q, k, v, gcol, grow, *, chunk):
    bsz, seq, dmix = q.shape
    dh = dmix // MLSTM_HEADS
    ng = 2 * MLSTM_HEADS
    tile = pl.BlockSpec((None, chunk, dmix), lambda b, c: (b, c, 0))
    return pl.pallas_call(
        functools.partial(_mlstm_scan_body, chunk=chunk, dh=dh),
        grid=(bsz, seq // chunk),
        in_specs=[tile, tile, tile,
                  pl.BlockSpec((None, chunk, ng), lambda b, c: (b, c, 0)),
                  pl.BlockSpec((None, ng, chunk), lambda b, c: (b, 0, c))],
        out_specs=tile,
        out_shape=jax.ShapeDtypeStruct((bsz, seq, dmix), BF16),
        scratch_shapes=[pltpu.VMEM((MLSTM_HEADS, dh, dh), F32),
                        pltpu.VMEM((MLSTM_HEADS, 1, dh), F32),
                        pltpu.VMEM((MLSTM_HEADS, 8, LANES), F32)],
        compiler_params=_params("parallel", "arbitrary"),
        name="mlstm_scan",
    )(q, k, v, gcol, grow)


def _silu(z):
    return z * _sigmoid(z)


def _memory_attention_into(y_ref, qm, kv_ref, z_attn, col0):
    dx = XATTN_HEADS * XATTN_HEAD_DIM
    for h in range(XATTN_HEADS):
        lo, hi = h * XATTN_HEAD_DIM, (h + 1) * XATTN_HEAD_DIM
        s = _dot_nt(qm[:, lo:hi], kv_ref[:, lo:hi]) * (XATTN_HEAD_DIM ** -0.5)
        p = jnp.exp(s - jnp.max(s, axis=-1, keepdims=True))
        o = _dot(_bf(p), kv_ref[:, dx + lo:dx + hi]) / jnp.sum(p, axis=-1, keepdims=True)
        y_ref[:, col0 + lo:col0 + hi] = _bf(o * _silu(_f32(z_attn[:, lo:hi])))


def _mlstm_out_body(hs_ref, xc_ref, qm_ref, z_ref, x_ref, kv_ref, wo_ref, mg_ref, sk_ref,
                    o_ref, y_ref, *, dh):
    dmix = MLSTM_HEADS * dh
    for h in range(MLSTM_HEADS):
        lo, hi = h * dh, (h + 1) * dh
        seg = _f32(hs_ref[:, lo:hi])
        d = seg - jnp.mean(seg, axis=-1, keepdims=True)
        var = jnp.mean(d * d, axis=-1, keepdims=True)
        ymix = d * lax.rsqrt(var + MHLN_EPS) * mg_ref[:, lo:hi] + sk_ref[:, lo:hi] * _f32(xc_ref[:, lo:hi])
        y_ref[:, lo:hi] = _bf(ymix * _silu(_f32(z_ref[:, lo:hi])))
    _memory_attention_into(y_ref, qm_ref[...], kv_ref, z_ref[:, dmix:], dmix)
    o_ref[...] = x_ref[...] + _dot(y_ref[...], wo_ref[...])


def _mlstm_out(hs, xc, proj, x, kv, w_out, mhn_g, skip, *, tm):
    bsz, seq, dmix = hs.shape
    dmodel = x.shape[-1]
    dx = XATTN_HEADS * XATTN_HEAD_DIM
    dinner = dmix + dx
    nmem = kv.shape[1]
    full = lambda shape: pl.BlockSpec(shape, lambda b, s: (0,) * len(shape))
    tile = lambda w, j: pl.BlockSpec((None, tm, w), lambda b, s: (b, s, j))
    return pl.pallas_call(
        functools.partial(_mlstm_out_body, dh=dmix // MLSTM_HEADS),
        grid=(bsz, seq // tm),
        in_specs=[tile(dmix, 0), tile(dmix, 0),
                  tile(dx, dmix // dx), tile(dinner, 1), tile(dmodel, 0),
                  pl.BlockSpec((None, nmem, 2 * dx), lambda b, s: (b, 0, 0)),
                  full(w_out.shape), full((1, dmix)), full((1, dmix))],
        out_specs=tile(dmodel, 0),
        out_shape=jax.ShapeDtypeStruct((bsz, seq, dmodel), F32),
        scratch_shapes=[pltpu.VMEM((tm, dinner), BF16)],
        compiler_params=_params("parallel", "parallel"),
        name="mlstm_out",
    )(hs, xc, proj, proj, x, kv, w_out, mhn_g.reshape(1, dmix), skip.reshape(1, dmix))


def _split2_dot(x, e):
    hi = _bf(x)
    lo = _bf(x - hi.astype(F32))
    return _dot(hi, e) + _dot(lo, e)


def _head_sums(x, e_ref):
    tm = x.shape[0]
    nslab = x.shape[1] // LANES
    rows = jnp.concatenate([x[:, i * LANES:(i + 1) * LANES] for i in range(nslab)], axis=0)
    sums = _split2_dot(rows, e_ref[...])
    return jnp.concatenate([sums[i * tm:(i + 1) * tm] for i in range(nslab)], axis=1)


def _token_shift(x, carry_ref, mu):
    tm = x.shape[0]
    prev = pltpu.roll(x, 1, axis=0)
    first = lax.broadcasted_iota(jnp.int32, (tm, 1), 0) == 0
    prev = jnp.where(first, carry_ref[7:8, :], prev)
    carry_ref[...] = x[tm - 8:tm, :]
    return x + (prev - x) * mu


def _rwkv_pre_body(pr_ref, pk_ref, pv_ref, pl_ref, vf_ref, mur_ref, muk_ref, muv_ref, mul_ref,
                   wl2_ref, vecs_ref, e_ref,
                   r_ref, lw_ref, k_ref, v_ref, kk_ref, ab_ref, g_ref, bonus_ref,
                   cr_ref, ck_ref, cv_ref, cl_ref):
    @pl.when(pl.program_id(1) == 0)
    def _():
        for c in (cr_ref, ck_ref, cv_ref, cl_ref):
            c[...] = jnp.zeros(c.shape, F32)

    r = _token_shift(_f32(pr_ref[...]), cr_ref, mur_ref[...])
    k = _token_shift(_f32(pk_ref[...]), ck_ref, muk_ref[...])
    v = _token_shift(_f32(pv_ref[...]), cv_ref, muv_ref[...])
    lo = _token_shift(_f32(pl_ref[...]), cl_ref, mul_ref[...])
    w0, a0, v0, k_k, k_a, r_k = (vecs_ref[i:i + 1, :] for i in range(6))

    seg = lambda i: lo[:, i * LORA_SEG:(i + 1) * LORA_SEG]
    d = w0 + _dot(_bf(jnp.tanh(seg(0))), wl2_ref[0])
    lw_ref[...] = -math.exp(-0.5) * _sigmoid(d)
    a = _sigmoid(a0 + _dot(_bf(seg(1)), wl2_ref[1]))
    v = v + (_f32(vf_ref[...]) - v) * _sigmoid(v0 + _dot(_bf(seg(2)), wl2_ref[2]))
    g_ref[...] = _bf(_dot(_bf(_sigmoid(seg(3))), wl2_ref[3]))

    kk = k * k_k
    kk = kk / jnp.maximum(jnp.sqrt(_head_sums(kk * kk, e_ref)), L2_EPS)
    k2 = k * (1.0 + (a - 1.0) * k_a)
    r_ref[...] = _bf(r)
    k_ref[...] = _bf(k2)
    v_ref[...] = _bf(v)
    kk_ref[...] = _bf(kk)
    ab_ref[...] = _bf(kk * a)
    bonus_ref[...] = _bf(_head_sums(r * k2 * r_k, e_ref) * v)


def _rwkv_pre(proj, v_first, mus, wl2, vecs, e2, *, tm, lora_block):
    bsz, seq, dmix = v_first.shape
    lw = 4 * LORA_SEG
    full = lambda shape: pl.BlockSpec(shape, lambda b, s: (0,) * len(shape))
    tile = lambda w, j: pl.BlockSpec((None, tm, w), lambda b, s: (b, s, j))
    act = jax.ShapeDtypeStruct((bsz, seq, dmix), BF16)
    log_decay = jax.ShapeDtypeStruct((bsz, seq, dmix), F32)
    return pl.pallas_call(
        _rwkv_pre_body,
        grid=(bsz, seq // tm),
        in_specs=[tile(dmix, 0), tile(dmix, 1), tile(dmix, 2), tile(lw, lora_block), tile(dmix, 0),
                  full((1, dmix)), full((1, dmix)), full((1, dmix)), full((1, lw)),
                  full(wl2.shape), full(vecs.shape), full(e2.shape)],
        out_specs=[tile(dmix, 0)] * 8,
        out_shape=[act, log_decay] + [act] * 6,
        scratch_shapes=[pltpu.VMEM((8, dmix), F32)] * 3 + [pltpu.VMEM((8, lw), F32)],
        compiler_params=_params("parallel", "arbitrary"),
        name="rwkv_pre",
    )(proj, proj, proj, proj, v_first, *mus, wl2, vecs, e2)


def _wkv7_body(r_ref, lw_ref, k_ref, v_ref, kk_ref, ab_ref, o_ref, h_ref, *, chunk, npair, group):
    @pl.when(pl.program_id(1) == 0)
    def _():
        h_ref[...] = jnp.zeros(h_ref.shape, F32)

    c2 = 2 * chunk
    row = lax.broadcasted_iota(jnp.int32, (chunk, chunk), 0)
    col = lax.broadcasted_iota(jnp.int32, (chunk, chunk), 1)
    tril = (col <= row).astype(F32)
    lw = lw_ref[...]
    g_inc = jnp.dot(tril, lw, precision=HIGHEST, preferred_element_type=F32)
    g_exc = g_inc - lw
    g_last = g_inc[chunk - 1:chunk, :]
    e_inc = jnp.exp(g_inc)
    e_neg = jnp.exp(-g_inc)
    e_end = jnp.exp(g_last - g_inc)
    kk = _f32(kk_ref[...])
    ab = _f32(ab_ref[...])
    kx = _f32(k_ref[...])
    a_t = -kk * jnp.exp(g_exc)
    b_t = ab * e_neg
    k_t = kx * e_neg
    r_t = _f32(r_ref[...]) * e_inc
    b_e = ab * e_end
    k_e = kx * e_end
    gam_l = jnp.exp(g_last)

    lane_head = lax.broadcasted_iota(jnp.int32, (c2, LANES), 1) // RWKV_HEAD_DIM
    row_head = lax.broadcasted_iota(jnp.int32, (c2, LANES), 0) // chunk
    head_mask = lane_head == row_head
    ri = lax.broadcasted_iota(jnp.int32, (c2, c2), 0)
    ci = lax.broadcasted_iota(jnp.int32, (c2, c2), 1)
    same = (ri // chunk) == (ci // chunk)
    tp = lax.broadcasted_iota(jnp.int32, (chunk, c2), 0)
    sp = lax.broadcasted_iota(jnp.int32, (chunk, c2), 1) % chunk
    strict = sp < tp
    incl = sp <= tp
    eye_p = (sp == tp).astype(F32)
    li = lax.broadcasted_iota(jnp.int32, (LANES, LANES), 0)
    lj = lax.broadcasted_iota(jnp.int32, (LANES, LANES), 1)
    lane_eye = (li == lj).astype(F32)
    lane_same = (li // RWKV_HEAD_DIM) == (lj // RWKV_HEAD_DIM)

    def stack(x):
        return _bf(jnp.where(head_mask, jnp.concatenate([x, x], axis=0), 0.0))

    def blocks(x):
        return _bf(jnp.where(same, jnp.concatenate([x, x], axis=0), 0.0))

    def cat0(*xs):
        return jnp.concatenate(xs, axis=0)

    def cat1(*xs):
        return jnp.concatenate(xs, axis=1)

    each = lambda f, *xs: [f(*t) for t in zip(*xs)]
    nlevel = int(math.log2(chunk))
    for g0 in range(0, npair, group):
        sls = [slice(p * LANES, (p + 1) * LANES) for p in range(g0, g0 + group)]
        v_s = [stack(_f32(v_ref[:, sl])) for sl in sls]
        sc = [_dot_nt(_bf(cat0(a_t[:, sl], r_t[:, sl])), cat0(stack(b_t[:, sl]), stack(k_t[:, sl]))) for sl in sls]
        d_ab = [jnp.where(strict, s[:chunk, :c2], 0.0) for s in sc]
        d_ak = [_bf(jnp.where(strict, s[:chunk, c2:], 0.0)) for s in sc]
        d_rb = [_bf(jnp.where(incl, s[chunk:, :c2], 0.0)) for s in sc]
        d_rk = [_bf(jnp.where(incl, s[chunk:, c2:], 0.0)) for s in sc]
        t_inv = [eye_p + d for d in d_ab]
        d_pow = d_ab
        for level in range(nlevel):
            rhs = each(blocks, d_pow)
            if level == 0:
                d_pow = each(lambda d, w: _dot(_bf(d), w), d_pow, rhs)
            elif level < nlevel - 1:
                both = each(lambda d, t, w: _dot(_bf(cat0(d, t)), w), d_pow, t_inv, rhs)
                d_pow = [x[:chunk] for x in both]
                t_inv = each(lambda t, x: t + x[chunk:], t_inv, both)
            else:
                t_inv = each(lambda t, w: t + _dot(_bf(t), w), t_inv, rhs)
        akv = each(_dot, d_ak, v_s)
        a_s = [stack(a_t[:, sl]) for sl in sls]
        wu = each(lambda t, a, x: _dot(_bf(t), cat1(a, stack(x))), t_inv, a_s, akv)
        ry = each(lambda drb, x: _dot(drb, cat1(stack(x[:, :LANES]), stack(x[:, LANES:]))), d_rb, wu)
        y0 = each(lambda x, drk, v: x[:, LANES:] + _dot(drk, v), ry, d_rk, v_s)
        rp = [r_t[:, sl] + x[:, :LANES] for sl, x in zip(sls, ry)]
        pq = [_dot_tn(_bf(b_e[:, sl]), _bf(x)) for sl, x in zip(sls, wu)]
        kv = [_dot_tn(_bf(k_e[:, sl]), v_ref[:, sl]) for sl in sls]
        p_mat = [lane_eye * gam_l[:, sl] + jnp.where(lane_same, x[:, :LANES], 0.0) for sl, x in zip(sls, pq)]
        q_mat = each(lambda x, y: jnp.where(lane_same, x[:, LANES:] + y, 0.0), pq, kv)
        for i, sl in enumerate(sls):
            yh = _dot(_bf(cat0(rp[i], p_mat[i])), _bf(h_ref[g0 + i]))
            o_ref[:, sl] = _bf(yh[:chunk] + y0[i])
            h_ref[g0 + i] = yh[chunk:] + q_mat[i]


def _wkv7(r, lw, k, v, kk, ab, *, chunk):
    bsz, seq, dmix = r.shape
    npair = dmix // LANES
    tile = pl.BlockSpec((None, chunk, dmix), lambda b, c: (b, c, 0))
    return pl.pallas_call(
        functools.partial(_wkv7_body, chunk=chunk, npair=npair, group=WKV_PAIR_GROUP),
        grid=(bsz, seq // chunk),
        in_specs=[tile] * 6,
        out_specs=tile,
        out_shape=jax.ShapeDtypeStruct((bsz, seq, dmix), BF16),
        scratch_shapes=[pltpu.VMEM((npair, LANES, LANES), F32)],
        compiler_params=_params("parallel", "arbitrary"),
        name="wkv7",
    )(r, lw, k, v, kk, ab)


def _rwkv_out_body(o_ref, g_ref, bonus_ref, qm_ref, za_ref, zb_ref, x_ref, kv_ref, wo_ref,
                   lg_ref, lb_ref, fg_ref, e_ref, out_ref, y_ref, *, dmix):
    half = za_ref.shape[1]
    inv_n = 1.0 / RWKV_HEAD_DIM
    for i in range(dmix // LANES):
        sl = slice(i * LANES, (i + 1) * LANES)
        seg = _f32(o_ref[:, sl])
        d = seg - _split2_dot(seg, e_ref[...]) * inv_n
        var = _split2_dot(d * d, e_ref[...]) * inv_n
        ymix = ((d * lax.rsqrt(var + RWKV_GN_EPS) * lg_ref[:, sl] + lb_ref[:, sl] + _f32(bonus_ref[:, sl]))
                * _f32(g_ref[:, sl]))
        z = za_ref[:, sl] if (i + 1) * LANES <= half else zb_ref[:, i * LANES - half:(i + 1) * LANES - half]
        y_ref[:, sl] = _bf(ymix * _silu(_f32(z)))
    _memory_attention_into(y_ref, qm_ref[...], kv_ref, zb_ref[:, dmix - half:], dmix)
    xn = x_ref[...] + _dot(y_ref[...], wo_ref[...])
    ms = jnp.mean(xn * xn, axis=-1, keepdims=True)
    out_ref[...] = xn * lax.rsqrt(ms + RMS_EPS) * fg_ref[...]


def _rwkv_out(o, g, bonus, proj, x, kv, w_out, lnx_g, lnx_b, final_g, e2, *, tm, qm_block, z_block):
    bsz, seq, dmix = o.shape
    dmodel = x.shape[-1]
    dx = XATTN_HEADS * XATTN_HEAD_DIM
    dinner = dmix + dx
    half = dinner // 2
    nmem = kv.shape[1]
    full = lambda shape: pl.BlockSpec(shape, lambda b, s: (0,) * len(shape))
    tile = lambda w, j: pl.BlockSpec((None, tm, w), lambda b, s: (b, s, j))
    return pl.pallas_call(
        functools.partial(_rwkv_out_body, dmix=dmix),
        grid=(bsz, seq // tm),
        in_specs=[tile(dmix, 0), tile(dmix, 0), tile(dmix, 0),
                  tile(dx, qm_block), tile(half, z_block), tile(half, z_block + 1), tile(dmodel, 0),
                  pl.BlockSpec((None, nmem, 2 * dx), lambda b, s: (b, 0, 0)),
                  full(w_out.shape), full((1, dmix)), full((1, dmix)), full((1, dmodel)), full(e2.shape)],
        out_specs=tile(dmodel, 0),
        out_shape=jax.ShapeDtypeStruct((bsz, seq, dmodel), F32),
        scratch_shapes=[pltpu.VMEM((tm, dinner), BF16)],
        compiler_params=_params("parallel", "parallel"),
        name="rwkv_out",
    )(o, g, bonus, proj, proj, proj, x, kv, w_out, lnx_g.reshape(1, dmix), lnx_b.reshape(1, dmix),
      final_g.reshape(1, dmodel), e2)


def _blockdiag_tiles(w, tile):
    nb, blk, _ = w.shape
    rows = w.reshape(nb * blk // tile, tile, blk)
    idx = jnp.arange(tile)
    on_diag = (idx[:, None] // blk) == (idx[None, :] // blk)
    return jnp.where(on_diag, jnp.tile(rows, (1, 1, tile // blk)), 0.0)


def _pad_rows(w, rows):
    return jnp.pad(w, ((0, rows - w.shape[0]), (0, 0)))


def _pad_cols(w, cols):
    return jnp.pad(w, ((0, 0), (0, cols - w.shape[1])))


def kernel(x, mem, norm_g, mem_norm_g, mem_kv_w, w_out, ml_w_in, ml_conv_w, ml_conv_b, ml_wq, ml_wk, ml_wv, ml_w_gate, ml_b_gate, ml_mhn_g, ml_skip, rw_w_in, rw_mu, rw_w_lora2, rw_w0, rw_a_lora2, rw_a0, rw_v_lora2, rw_v0, rw_g_lora2, rw_k_k, rw_k_a, rw_r_k, rw_lnx_g, rw_lnx_b, final_g):
    bsz, seq, dmodel = x.shape
    nmem = mem.shape[1]
    dmix = ml_conv_w.shape[-1]
    dx = XATTN_HEADS * XATTN_HEAD_DIM
    ntok = bsz * seq
    mem2d = mem.reshape(bsz * nmem, dmodel)

    kv0 = _rms_matmul(mem2d, mem_norm_g[0], _bf(mem_kv_w[0]), tm=512, tn=2 * dx,
                      out_dtype=BF16).reshape(bsz, nmem, 2 * dx)
    proj0 = _rms_matmul(x.reshape(ntok, dmodel), norm_g[0], _bf(ml_w_in[0]), tm=512, tn=1024,
                        out_dtype=BF16).reshape(bsz, seq, -1)
    wqk = jnp.concatenate([_blockdiag_tiles(ml_wq[0], MXU_DIM), _blockdiag_tiles(ml_wk[0], MXU_DIM)], axis=-1)
    wv = _blockdiag_tiles(ml_wv[0], MXU_DIM)
    wg = ml_w_gate[0].reshape(3, dmix, -1)
    xc, q, k, v0, gcol, grow = _mlstm_pre(
        proj0, ml_conv_w[0], ml_conv_b[0], _bf(wqk), _bf(wv),
        _bf(jnp.pad(wg, ((0, 0), (0, 0), (0, LANES - wg.shape[-1])))),
        _bf(jnp.pad(wg.transpose(0, 2, 1), ((0, 0), (0, 16 - wg.shape[-1]), (0, 0)))),
        ml_b_gate[0], ts=256)
    hs = _mlstm_scan(q, k, v0, gcol, grow, chunk=MLSTM_CHUNK)
    x1 = _mlstm_out(hs, xc, proj0, x, kv0, _bf(w_out[0]), ml_mhn_g[0], ml_skip[0], tm=256)

    w_in = rw_w_in[0]
    mu = rw_mu[0]
    cuts = [0, 3 * dmix]
    for rank in (DECAY_RANK, ICLR_RANK, VRES_RANK, GATE_RANK):
        cuts.append(cuts[-1] + rank)
    lora_w = jnp.concatenate([_pad_cols(w_in[:, cuts[i]:cuts[i + 1]], LORA_SEG) for i in range(1, 5)], axis=1)
    lora_mu = jnp.concatenate([jnp.pad(mu[cuts[i]:cuts[i + 1]], (0, LORA_SEG - (cuts[i + 1] - cuts[i])))
                               for i in range(1, 5)])
    w_packed = jnp.concatenate([w_in[:, :3 * dmix], w_in[:, cuts[-1]:], lora_w], axis=1)
    qm_block = (3 * dmix) // dx
    z_block = (3 * dmix + dx) // ((dmix + dx) // 2)
    lora_block = (3 * dmix + dx + dmix + dx) // (4 * LORA_SEG)

    kv1 = _rms_matmul(mem2d, mem_norm_g[1], _bf(mem_kv_w[1]), tm=512, tn=2 * dx,
                      out_dtype=BF16).reshape(bsz, nmem, 2 * dx)
    proj1 = _rms_matmul(x1.reshape(ntok, dmodel), norm_g[1], _bf(w_packed), tm=512, tn=1536,
                        out_dtype=BF16).reshape(bsz, seq, -1)
    wl2 = _bf(jnp.stack([_pad_rows(rw_w_lora2[0], LORA_SEG), _pad_rows(rw_a_lora2[0], LORA_SEG),
                         _pad_rows(rw_v_lora2[0], LORA_SEG), _pad_rows(rw_g_lora2[0], LORA_SEG)]))
    vecs = jnp.stack([rw_w0[0], rw_a0[0], rw_v0[0], rw_k_k[0], rw_k_a[0], rw_r_k[0].reshape(-1),
                      jnp.zeros_like(rw_w0[0]), jnp.zeros_like(rw_w0[0])])
    lane = jnp.arange(LANES) // RWKV_HEAD_DIM
    e2 = _bf(lane[:, None] == lane[None, :])
    mus = (mu[:dmix].reshape(1, -1), mu[dmix:2 * dmix].reshape(1, -1), mu[2 * dmix:3 * dmix].reshape(1, -1),
           lora_mu.reshape(1, -1))
    r, lw, k2, v1, kk, ab, g, bonus = _rwkv_pre(proj1, v0, mus, wl2, vecs, e2, tm=256, lora_block=lora_block)
    o = _wkv7(r, lw, k2, v1, kk, ab, chunk=WKV_CHUNK)
    return _rwkv_out(o, g, bonus, proj1, x1, kv1, _bf(w_out[1]), rw_lnx_g[0], rw_lnx_b[0], final_g, e2,
                     tm=256, qm_block=qm_block, z_block=z_block)
```

```python
import functools
import math

import jax
import jax.numpy as jnp
from jax import lax
from jax.experimental import pallas as pl
from jax.experimental.pallas import tpu as pltpu

F32 = jnp.float32
BF16 = jnp.bfloat16
HIGHEST = lax.Precision.HIGHEST

LANES = 128
MXU_DIM = 256
VMEM_LIMIT = 48 * 1024 * 1024

XATTN_HEADS = 4
XATTN_HEAD_DIM = 128
MLSTM_HEADS = 4
MLSTM_CONV = 4
QKV_BLOCK = 4
RWKV_HEAD_DIM = 64
DECAY_RANK, ICLR_RANK, VRES_RANK, GATE_RANK = 64, 64, 32, 128
LORA_SEG = 128

RMS_EPS = 1e-6
MHLN_EPS = 1e-5
RWKV_GN_EPS = 64e-5
L2_EPS = 1e-12

MLSTM_CHUNK = 256
WKV_CHUNK = 64
WKV_PAIR_GROUP = 12


def _params(*sem):
    return pltpu.CompilerParams(dimension_semantics=sem, vmem_limit_bytes=VMEM_LIMIT)


def _sigmoid(x):
    return 0.5 * jnp.tanh(0.5 * x) + 0.5


def _log_sigmoid(x):
    return jnp.minimum(x, 0.0) - jnp.log1p(jnp.exp(-jnp.abs(x)))


def _dot(a, b):
    return jnp.dot(a, b, preferred_element_type=F32)


def _dot_nt(a, b):
    return lax.dot_general(a, b, (((1,), (1,)), ((), ())), preferred_element_type=F32)


def _dot_tn(a, b):
    return lax.dot_general(a, b, (((0,), (0,)), ((), ())), preferred_element_type=F32)


def _bf(x):
    return x.astype(BF16)


def _f32(x):
    return x.astype(F32)


def _rms_matmul_body(x_ref, g_ref, w_ref, o_ref, h_ref):
    @pl.when(pl.program_id(1) == 0)
    def _():
        x = x_ref[...]
        ms = jnp.mean(x * x, axis=-1, keepdims=True)
        h_ref[...] = _bf(x * lax.rsqrt(ms + RMS_EPS) * g_ref[...])

    o_ref[...] = _dot(h_ref[...], w_ref[...]).astype(o_ref.dtype)


def _rms_matmul(x2d, g, w, *, tm, tn, out_dtype):
    m, d = x2d.shape
    n = w.shape[1]
    return pl.pallas_call(
        _rms_matmul_body,
        grid=(m // tm, n // tn),
        in_specs=[pl.BlockSpec((tm, d), lambda i, j: (i, 0)),
                  pl.BlockSpec((1, d), lambda i, j: (0, 0)),
                  pl.BlockSpec((d, tn), lambda i, j: (0, j))],
        out_specs=pl.BlockSpec((tm, tn), lambda i, j: (i, j)),
        out_shape=jax.ShapeDtypeStruct((m, n), out_dtype),
        scratch_shapes=[pltpu.VMEM((tm, d), BF16)],
        compiler_params=_params("parallel", "arbitrary"),
        name="rms_matmul",
    )(x2d, g.reshape(1, d), w)


def _mlstm_pre_body(u_ref, cw_ref, cb_ref, wqk_ref, wv_ref, wg_ref, wgt_ref, bgr_ref, bgc_ref,
                    xc_ref, q_ref, k_ref, v_ref, gcol_ref, grow_ref, ext_ref, *, ts, nblk):
    s = pl.program_id(1)

    @pl.when(s == 0)
    def _():
        ext_ref[0:8, :] = jnp.zeros((8, ext_ref.shape[1]), F32)

    @pl.when(s > 0)
    def _():
        ext_ref[0:8, :] = ext_ref[ts:ts + 8, :]

    ub = u_ref[...]
    u = _f32(ub)
    ext_ref[8:8 + ts, :] = u
    acc = cb_ref[...] + cw_ref[MLSTM_CONV - 1:MLSTM_CONV, :] * u
    for j in range(1, MLSTM_CONV):
        acc = acc + cw_ref[MLSTM_CONV - 1 - j:MLSTM_CONV - j, :] * ext_ref[8 - j:8 - j + ts, :]
    xc = acc * _sigmoid(acc)
    xcb = _bf(xc)
    xc_ref[...] = xcb

    for b in range(nblk):
        lo, hi = b * MXU_DIM, (b + 1) * MXU_DIM
        qk = _dot(xcb[:, lo:hi], wqk_ref[b])
        q_ref[:, lo:hi] = _bf(qk[:, :MXU_DIM])
        k_ref[:, lo:hi] = _bf(qk[:, MXU_DIM:])
        v_ref[:, lo:hi] = _bf(_dot(ub[:, lo:hi], wv_ref[b]))

    qb, kb, vb = q_ref[...], k_ref[...], v_ref[...]
    gc = _dot(qb, wg_ref[0]) + _dot(kb, wg_ref[1]) + _dot(vb, wg_ref[2]) + bgr_ref[...]
    lane = lax.broadcasted_iota(jnp.int32, gc.shape, 1)
    gc = jnp.where(lane < MLSTM_HEADS, gc, _log_sigmoid(gc))
    gcol_ref[...] = gc[:, :2 * MLSTM_HEADS]

    gr = _dot_nt(wgt_ref[0], qb) + _dot_nt(wgt_ref[1], kb) + _dot_nt(wgt_ref[2], vb)
    gr = gr[:2 * MLSTM_HEADS, :] + bgc_ref[...]
    row = lax.broadcasted_iota(jnp.int32, gr.shape, 0)
    grow_ref[...] = jnp.where(row < MLSTM_HEADS, gr, _log_sigmoid(gr))


def _mlstm_pre(proj, conv_w, conv_b, wqk, wv, wg, wgt, bg, *, ts):
    bsz, seq, _ = proj.shape
    dmix = conv_w.shape[1]
    nblk = dmix // MXU_DIM
    ng = 2 * MLSTM_HEADS
    full = lambda shape: pl.BlockSpec(shape, lambda b, s: (0,) * len(shape))
    tile = pl.BlockSpec((None, ts, dmix), lambda b, s: (b, s, 0))
    act = jax.ShapeDtypeStruct((bsz, seq, dmix), BF16)
    return pl.pallas_call(
        functools.partial(_mlstm_pre_body, ts=ts, nblk=nblk),
        grid=(bsz, seq // ts),
        in_specs=[tile, full(conv_w.shape), full((1, dmix)), full(wqk.shape), full(wv.shape),
                  full(wg.shape), full(wgt.shape), full((1, LANES)), full((ng, 1))],
        out_specs=[tile, tile, tile, tile,
                   pl.BlockSpec((None, ts, ng), lambda b, s: (b, s, 0)),
                   pl.BlockSpec((None, ng, ts), lambda b, s: (b, 0, s))],
        out_shape=[act, act, act, act,
                   jax.ShapeDtypeStruct((bsz, seq, ng), F32),
                   jax.ShapeDtypeStruct((bsz, ng, seq), F32)],
        scratch_shapes=[pltpu.VMEM((ts + 8, dmix), F32)],
        compiler_params=_params("parallel", "arbitrary"),
        name="mlstm_pre",
    )(proj, conv_w, conv_b.reshape(1, dmix), wqk, wv, wg, wgt,
      jnp.pad(bg, (0, LANES - ng)).reshape(1, LANES), bg.reshape(ng, 1))


def _mlstm_scan_body(q_ref, k_ref, v_ref, gcol_ref, grow_ref, h_ref, c_ref, n_ref, m_ref, *, chunk, dh):
    @pl.when(pl.program_id(1) == 0)
    def _():
        c_ref[...] = jnp.zeros(c_ref.shape, F32)
        n_ref[...] = jnp.zeros(n_ref.shape, F32)
        m_ref[...] = jnp.zeros(m_ref.shape, F32)

    gc = gcol_ref[...]
    gr = grow_ref[...]
    row = lax.broadcasted_iota(jnp.int32, (chunk, chunk), 0)
    col = lax.broadcasted_iota(jnp.int32, (chunk, chunk), 1)
    causal = col <= row
    tril = causal.astype(F32)
    bcol_all = jnp.dot(tril, gc, precision=HIGHEST, preferred_element_type=F32)
    brow_all = lax.dot_general(gr, tril, (((1,), (1,)), ((), ())), precision=HIGHEST,
                               preferred_element_type=F32)
    scale = dh ** -0.5
    for h in range(MLSTM_HEADS):
        lo, hi = h * dh, (h + 1) * dh
        i_col = gc[:, h:h + 1]
        b_col = bcol_all[:, MLSTM_HEADS + h:MLSTM_HEADS + h + 1]
        i_row = gr[h:h + 1, :]
        b_row = brow_all[MLSTM_HEADS + h:MLSTM_HEADS + h + 1, :]
        m_old = m_ref[h][0:1, 0:1]
        d_mat = jnp.where(causal, b_col - b_row + i_row, -jnp.inf)
        inter = b_col + m_old
        m_t = jnp.maximum(jnp.max(d_mat, axis=-1, keepdims=True), inter)
        qb = q_ref[:, lo:hi]
        qh = _f32(qb)
        kh = _f32(k_ref[:, lo:hi]) * scale
        kb, vb = _bf(kh), v_ref[:, lo:hi]
        s = _dot_nt(qb, kb) * jnp.exp(d_mat - m_t)
        g_in = jnp.exp(inter - m_t)
        c_old = c_ref[h]
        n_old = n_ref[h]
        num = _dot(_bf(s), vb) + g_in * _dot(qb, _bf(c_old))
        den = jnp.sum(s, axis=-1, keepdims=True) + g_in * jnp.sum(qh * n_old, axis=-1, keepdims=True)
        h_ref[:, lo:hi] = _bf(num / jnp.maximum(jnp.abs(den), jnp.exp(-m_t)))
        b_last = b_col[chunk - 1:chunk, :]
        w_s = b_last - b_col + i_col
        m_new = jnp.maximum(b_last + m_old, jnp.max(w_s, axis=0, keepdims=True))
        kw = kh * jnp.exp(w_s - m_new)
        g_old = jnp.exp(b_last + m_old - m_new)
        c_ref[h] = g_old * c_old + _dot_tn(_bf(kw), vb)
        n_ref[h] = g_old * n_old + jnp.sum(kw, axis=0, keepdims=True)
        m_ref[h] = jnp.broadcast_to(m_new, m_ref.shape[1:])


def _mlstm_scan(q, k, v, gcol, grow, *, chunk):
    bsz, seq, dmix = q.shape
    dh = dmix // MLSTM_HEADS
    ng = 2 * MLSTM_HEADS
    tile = pl.BlockSpec((None, chunk, dmix), lambda b, c: (b, c, 0))
    return pl.pallas_call(
        functools.partial(_mlstm_scan_body, chunk=chunk, dh=dh),
        grid=(bsz, seq // chunk),
        in_specs=[tile, tile, tile,
                  pl.BlockSpec((None, chunk, ng), lambda b, c: (b, c, 0)),
                  pl.BlockSpec((None, ng, chunk), lambda b, c: (b, 0, c))],
        out_specs=tile,
        out_shape=jax.ShapeDtypeStruct((bsz, seq, dmix), BF16),
        scratch_shapes=[pltpu.VMEM((MLSTM_HEADS, dh, dh), F32),
                        pltpu.VMEM((MLSTM_HEADS, 1, dh), F32),
                        pltpu.VMEM((MLSTM_HEADS, 8, LANES), F32)],
        compiler_params=_params("parallel", "arbitrary"),
        name="mlstm_scan",
    )(q, k, v, gcol, grow)


def _silu(z):
    return z * _sigmoid(z)


def _memory_attention_into(y_ref, qm, kv_ref, z_attn, col0):
    dx = XATTN_HEADS * XATTN_HEAD_DIM
    for h in range(XATTN_HEADS):
        lo, hi = h * XATTN_HEAD_DIM, (h + 1) * XATTN_HEAD_DIM
        s = _dot_nt(qm[:, lo:hi], kv_ref[:, lo:hi]) * (XATTN_HEAD_DIM ** -0.5)
        p = jnp.exp(s - jnp.max(s, axis=-1, keepdims=True))
        o = _dot(_bf(p), kv_ref[:, dx + lo:dx + hi]) / jnp.sum(p, axis=-1, keepdims=True)
        y_ref[:, col0 + lo:col0 + hi] = _bf(o * _silu(_f32(z_attn[:, lo:hi])))


def _mlstm_out_body(hs_ref, xc_ref, qm_ref, z_ref, x_ref, kv_ref, wo_ref, mg_ref, sk_ref,
                    o_ref, y_ref, *, dh):
    dmix = MLSTM_HEADS * dh
    for h in range(MLSTM_HEADS):
        lo, hi = h * dh, (h + 1) * dh
        seg = _f32(hs_ref[:, lo:hi])
        d = seg - jnp.mean(seg, axis=-1, keepdims=True)
        var = jnp.mean(d * d, axis=-1, keepdims=True)
        ymix = d * lax.rsqrt(var + MHLN_EPS) * mg_ref[:, lo:hi] + sk_ref[:, lo:hi] * _f32(xc_ref[:, lo:hi])
        y_ref[:, lo:hi] = _bf(ymix * _silu(_f32(z_ref[:, lo:hi])))
    _memory_attention_into(y_ref, qm_ref[...], kv_ref, z_ref[:, dmix:], dmix)
    o_ref[...] = x_ref[...] + _dot(y_ref[...], wo_ref[...])


def _mlstm_out(hs, xc, proj, x, kv, w_out, mhn_g, skip, *, tm):
    bsz, seq, dmix = hs.shape
    dmodel = x.shape[-1]
    dx = XATTN_HEADS * XATTN_HEAD_DIM
    dinner = dmix + dx
    nmem = kv.shape[1]
    full = lambda shape: pl.BlockSpec(shape, lambda b, s: (0,) * len(shape))
    tile = lambda w, j: pl.BlockSpec((None, tm, w), lambda b, s: (b, s, j))
    return pl.pallas_call(
        functools.partial(_mlstm_out_body, dh=dmix // MLSTM_HEADS),
        grid=(bsz, seq // tm),
        in_specs=[tile(dmix, 0), tile(dmix, 0),
                  tile(dx, dmix // dx), tile(dinner, 1), tile(dmodel, 0),
                  pl.BlockSpec((None, nmem, 2 * dx), lambda b, s: (b, 0, 0)),
                  full(w_out.shape), full((1, dmix)), full((1, dmix))],
        out_specs=tile(dmodel, 0),
        out_shape=jax.ShapeDtypeStruct((bsz, seq, dmodel), F32),
        scratch_shapes=[pltpu.VMEM((tm, dinner), BF16)],
        compiler_params=_params("parallel", "parallel"),
        name="mlstm_out",
    )(hs, xc, proj, proj, x, kv, w_out, mhn_g.reshape(1, dmix), skip.reshape(1, dmix))


def _head_dot(x, e):
    return _dot(_bf(x), e)


def _head_sums(x, e_ref):
    tm = x.shape[0]
    nslab = x.shape[1] // LANES
    rows = jnp.concatenate([x[:, i * LANES:(i + 1) * LANES] for i in range(nslab)], axis=0)
    sums = _head_dot(rows, e_ref[...])
    return jnp.concatenate([sums[i * tm:(i + 1) * tm] for i in range(nslab)], axis=1)


def _token_shift(x, carry_ref, mu):
    tm = x.shape[0]
    prev = pltpu.roll(x, 1, axis=0)
    first = lax.broadcasted_iota(jnp.int32, (tm, 1), 0) == 0
    prev = jnp.where(first, carry_ref[7:8, :], prev)
    carry_ref[...] = x[tm - 8:tm, :]
    return x + (prev - x) * mu


def _rwkv_pre_body(pr_ref, pk_ref, pv_ref, pl_ref, vf_ref, mur_ref, muk_ref, muv_ref, mul_ref,
                   wl2_ref, vecs_ref, e_ref,
                   r_ref, lw_ref, k_ref, v_ref, kk_ref, ab_ref, g_ref, bonus_ref,
                   cr_ref, ck_ref, cv_ref, cl_ref):
    @pl.when(pl.program_id(1) == 0)
    def _():
        for c in (cr_ref, ck_ref, cv_ref, cl_ref):
            c[...] = jnp.zeros(c.shape, F32)

    r = _token_shift(_f32(pr_ref[...]), cr_ref, mur_ref[...])
    k = _token_shift(_f32(pk_ref[...]), ck_ref, muk_ref[...])
    v = _token_shift(_f32(pv_ref[...]), cv_ref, muv_ref[...])
    lo = _token_shift(_f32(pl_ref[...]), cl_ref, mul_ref[...])
    w0, a0, v0, k_k, k_a, r_k = (vecs_ref[i:i + 1, :] for i in range(6))

    seg = lambda i: lo[:, i * LORA_SEG:(i + 1) * LORA_SEG]
    d = w0 + _dot(_bf(jnp.tanh(seg(0))), wl2_ref[0])
    lw_ref[...] = -math.exp(-0.5) * _sigmoid(d)
    a = _sigmoid(a0 + _dot(_bf(seg(1)), wl2_ref[1]))
    v = v + (_f32(vf_ref[...]) - v) * _sigmoid(v0 + _dot(_bf(seg(2)), wl2_ref[2]))
    g_ref[...] = _bf(_dot(_bf(_sigmoid(seg(3))), wl2_ref[3]))

    kk = k * k_k
    kk = kk * lax.rsqrt(jnp.maximum(_head_sums(kk * kk, e_ref), L2_EPS * L2_EPS))
    k2 = k * (1.0 + (a - 1.0) * k_a)
    r_ref[...] = _bf(r)
    k_ref[...] = _bf(k2)
    v_ref[...] = _bf(v)
    kk_ref[...] = _bf(kk)
    ab_ref[...] = _bf(kk * a)
    bonus_ref[...] = _bf(_head_sums(r * k2 * r_k, e_ref) * v)


def _rwkv_pre(proj, v_first, mus, wl2, vecs, e2, *, tm, lora_block):
    bsz, seq, dmix = v_first.shape
    lw = 4 * LORA_SEG
    full = lambda shape: pl.BlockSpec(shape, lambda b, s: (0,) * len(shape))
    tile = lambda w, j: pl.BlockSpec((None, tm, w), lambda b, s: (b, s, j))
    act = jax.ShapeDtypeStruct((bsz, seq, dmix), BF16)
    log_decay = jax.ShapeDtypeStruct((bsz, seq, dmix), F32)
    return pl.pallas_call(
        _rwkv_pre_body,
        grid=(bsz, seq // tm),
        in_specs=[tile(dmix, 0), tile(dmix, 1), tile(dmix, 2), tile(lw, lora_block), tile(dmix, 0),
                  full((1, dmix)), full((1, dmix)), full((1, dmix)), full((1, lw)),
                  full(wl2.shape), full(vecs.shape), full(e2.shape)],
        out_specs=[tile(dmix, 0)] * 8,
        out_shape=[act, log_decay] + [act] * 6,
        scratch_shapes=[pltpu.VMEM((8, dmix), F32)] * 3 + [pltpu.VMEM((8, lw), F32)],
        compiler_params=_params("parallel", "arbitrary"),
        name="rwkv_pre",
    )(proj, proj, proj, proj, v_first, *mus, wl2, vecs, e2)


def _wkv7_body(r_ref, lw_ref, k_ref, v_ref, kk_ref, ab_ref, o_ref, h_ref, *, chunk, npair, group):
    @pl.when(pl.program_id(1) == 0)
    def _():
        h_ref[...] = jnp.zeros(h_ref.shape, F32)

    c2 = 2 * chunk
    row = lax.broadcasted_iota(jnp.int32, (chunk, chunk), 0)
    col = lax.broadcasted_iota(jnp.int32, (chunk, chunk), 1)
    tril = (col <= row).astype(F32)
    lw = lw_ref[...]
    g_inc = jnp.dot(tril, lw, precision=HIGHEST, preferred_element_type=F32)
    g_exc = g_inc - lw
    g_last = g_inc[chunk - 1:chunk, :]
    e_inc = jnp.exp(g_inc)
    e_neg = jnp.exp(-g_inc)
    e_end = jnp.exp(g_last - g_inc)
    kk = _f32(kk_ref[...])
    ab = _f32(ab_ref[...])
    kx = _f32(k_ref[...])
    a_t = -kk * jnp.exp(g_exc)
    b_t = ab * e_neg
    k_t = kx * e_neg
    r_t = _f32(r_ref[...]) * e_inc
    b_e = ab * e_end
    k_e = kx * e_end
    gam_l = jnp.exp(g_last)

    lane_head = lax.broadcasted_iota(jnp.int32, (c2, LANES), 1) // RWKV_HEAD_DIM
    row_head = lax.broadcasted_iota(jnp.int32, (c2, LANES), 0) // chunk
    head_mask = lane_head == row_head
    ri = lax.broadcasted_iota(jnp.int32, (c2, c2), 0)
    ci = lax.broadcasted_iota(jnp.int32, (c2, c2), 1)
    same = (ri // chunk) == (ci // chunk)
    tp = lax.broadcasted_iota(jnp.int32, (chunk, c2), 0)
    sp = lax.broadcasted_iota(jnp.int32, (chunk, c2), 1) % chunk
    strict = sp < tp
    incl = sp <= tp
    eye_p = (sp == tp).astype(F32)
    li = lax.broadcasted_iota(jnp.int32, (LANES, LANES), 0)
    lj = lax.broadcasted_iota(jnp.int32, (LANES, LANES), 1)
    lane_eye = (li == lj).astype(F32)
    lane_same = (li // RWKV_HEAD_DIM) == (lj // RWKV_HEAD_DIM)

    def stack(x):
        return _bf(jnp.where(head_mask, jnp.concatenate([x, x], axis=0), 0.0))

    def blocks(x):
        return _bf(jnp.where(same, jnp.concatenate([x, x], axis=0), 0.0))

    def cat0(*xs):
        return jnp.concatenate(xs, axis=0)

    def cat1(*xs):
        return jnp.concatenate(xs, axis=1)

    each = lambda f, *xs: [f(*t) for t in zip(*xs)]
    nlevel = int(math.log2(chunk))
    for g0 in range(0, npair, group):
        sls = [slice(p * LANES, (p + 1) * LANES) for p in range(g0, g0 + group)]
        v_s = [stack(_f32(v_ref[:, sl])) for sl in sls]
        sc = [_dot_nt(_bf(cat0(a_t[:, sl], r_t[:, sl])), cat0(stack(b_t[:, sl]), stack(k_t[:, sl]))) for sl in sls]
        d_ab = [jnp.where(strict, s[:chunk, :c2], 0.0) for s in sc]
        d_ak = [_bf(jnp.where(strict, s[:chunk, c2:], 0.0)) for s in sc]
        d_rb = [_bf(jnp.where(incl, s[chunk:, :c2], 0.0)) for s in sc]
        d_rk = [_bf(jnp.where(incl, s[chunk:, c2:], 0.0)) for s in sc]
        t_inv = [eye_p + d for d in d_ab]
        d_pow = d_ab
        for level in range(nlevel):
            rhs = each(blocks, d_pow)
            if level == 0:
                d_pow = each(lambda d, w: _dot(_bf(d), w), d_pow, rhs)
            elif level < nlevel - 1:
                both = each(lambda d, t, w: _dot(_bf(cat0(d, t)), w), d_pow, t_inv, rhs)
                d_pow = [x[:chunk] for x in both]
                t_inv = each(lambda t, x: t + x[chunk:], t_inv, both)
            else:
                t_inv = each(lambda t, w: t + _dot(_bf(t), w), t_inv, rhs)
        akv = each(_dot, d_ak, v_s)
        a_s = [stack(a_t[:, sl]) for sl in sls]
        wu = each(lambda t, a, x: _dot(_bf(t), cat1(a, stack(x))), t_inv, a_s, akv)
        ry = each(lambda drb, x: _dot(drb, cat1(stack(x[:, :LANES]), stack(x[:, LANES:]))), d_rb, wu)
        y0 = each(lambda x, drk, v: x[:, LANES:] + _dot(drk, v), ry, d_rk, v_s)
        rp = [r_t[:, sl] + x[:, :LANES] for sl, x in zip(sls, ry)]
        pq = [_dot_tn(_bf(b_e[:, sl]), _bf(x)) for sl, x in zip(sls, wu)]
        kv = [_dot_tn(_bf(k_e[:, sl]), v_ref[:, sl]) for sl in sls]
        p_mat = [lane_eye * gam_l[:, sl] + jnp.where(lane_same, x[:, :LANES], 0.0) for sl, x in zip(sls, pq)]
        q_mat = each(lambda x, y: jnp.where(lane_same, x[:, LANES:] + y, 0.0), pq, kv)
        for i, sl in enumerate(sls):
            yh = _dot(_bf(cat0(rp[i], p_mat[i])), _bf(h_ref[g0 + i]))
            o_ref[:, sl] = _bf(yh[:chunk] + y0[i])
            h_ref[g0 + i] = yh[chunk:] + q_mat[i]


def _wkv7(r, lw, k, v, kk, ab, *, chunk):
    bsz, seq, dmix = r.shape
    npair = dmix // LANES
    tile = pl.BlockSpec((None, chunk, dmix), lambda b, c: (b, c, 0))
    return pl.pallas_call(
        functools.partial(_wkv7_body, chunk=chunk, npair=npair, group=WKV_PAIR_GROUP),
        grid=(bsz, seq // chunk),
        in_specs=[tile] * 6,
        out_specs=tile,
        out_shape=jax.ShapeDtypeStruct((bsz, seq, dmix), BF16),
        scratch_shapes=[pltpu.VMEM((npair, LANES, LANES), F32)],
        compiler_params=_params("parallel", "arbitrary"),
        name="wkv7",
    )(r, lw, k, v, kk, ab)


def _rwkv_out_body(o_ref, g_ref, bonus_ref, qm_ref, za_ref, zb_ref, x_ref, kv_ref, wo_ref,
                   lg_ref, lb_ref, fg_ref, e_ref, out_ref, y_ref, *, dmix):
    half = za_ref.shape[1]
    inv_n = 1.0 / RWKV_HEAD_DIM
    for i in range(dmix // LANES):
        sl = slice(i * LANES, (i + 1) * LANES)
        seg = _f32(o_ref[:, sl])
        d = seg - _head_dot(seg, e_ref[...]) * inv_n
        var = _head_dot(d * d, e_ref[...]) * inv_n
        ymix = ((d * lax.rsqrt(var + RWKV_GN_EPS) * lg_ref[:, sl] + lb_ref[:, sl] + _f32(bonus_ref[:, sl]))
                * _f32(g_ref[:, sl]))
        z = za_ref[:, sl] if (i + 1) * LANES <= half else zb_ref[:, i * LANES - half:(i + 1) * LANES - half]
        y_ref[:, sl] = _bf(ymix * _silu(_f32(z)))
    _memory_attention_into(y_ref, qm_ref[...], kv_ref, zb_ref[:, dmix - half:], dmix)
    xn = x_ref[...] + _dot(y_ref[...], wo_ref[...])
    ms = jnp.mean(xn * xn, axis=-1, keepdims=True)
    out_ref[...] = xn * lax.rsqrt(ms + RMS_EPS) * fg_ref[...]


def _rwkv_out(o, g, bonus, proj, x, kv, w_out, lnx_g, lnx_b, final_g, e2, *, tm, qm_block, z_block):
    bsz, seq, dmix = o.shape
    dmodel = x.shape[-1]
    dx = XATTN_HEADS * XATTN_HEAD_DIM
    dinner = dmix + dx
    half = dinner // 2
    nmem = kv.shape[1]
    full = lambda shape: pl.BlockSpec(shape, lambda b, s: (0,) * len(shape))
    tile = lambda w, j: pl.BlockSpec((None, tm, w), lambda b, s: (b, s, j))
    return pl.pallas_call(
        functools.partial(_rwkv_out_body, dmix=dmix),
        grid=(bsz, seq // tm),
        in_specs=[tile(dmix, 0), tile(dmix, 0), tile(dmix, 0),
                  tile(dx, qm_block), tile(half, z_block), tile(half, z_block + 1), tile(dmodel, 0),
                  pl.BlockSpec((None, nmem, 2 * dx), lambda b, s: (b, 0, 0)),
                  full(w_out.shape), full((1, dmix)), full((1, dmix)), full((1, dmodel)), full(e2.shape)],
        out_specs=tile(dmodel, 0),
        out_shape=jax.ShapeDtypeStruct((bsz, seq, dmodel), F32),
        scratch_shapes=[pltpu.VMEM((tm, dinner), BF16)],
        compiler_params=_params("parallel", "parallel"),
        name="rwkv_out",
    )(o, g, bonus, proj, proj, proj, x, kv, w_out, lnx_g.reshape(1, dmix), lnx_b.reshape(1, dmix),
      final_g.reshape(1, dmodel), e2)


def _blockdiag_tiles(w, tile):
    nb, blk, _ = w.shape
    rows = w.reshape(nb * blk // tile, tile, blk)
    idx = jnp.arange(tile)
    on_diag = (idx[:, None] // blk) == (idx[None, :] // blk)
    return jnp.where(on_diag, jnp.tile(rows, (1, 1, tile // blk)), 0.0)


def _pad_rows(w, rows):
    return jnp.pad(w, ((0, rows - w.shape[0]), (0, 0)))


def _pad_cols(w, cols):
    return jnp.pad(w, ((0, 0), (0, cols - w.shape[1])))


def kernel(x, mem, norm_g, mem_norm_g, mem_kv_w, w_out, ml_w_in, ml_conv_w, ml_conv_b, ml_wq, ml_wk, ml_wv, ml_w_gate, ml_b_gate, ml_mhn_g, ml_skip, rw_w_in, rw_mu, rw_w_lora2, rw_w0, rw_a_lora2, rw_a0, rw_v_lora2, rw_v0, rw_g_lora2, rw_k_k, rw_k_a, rw_r_k, rw_lnx_g, rw_lnx_b, final_g):
    bsz, seq, dmodel = x.shape
    nmem = mem.shape[1]
    dmix = ml_conv_w.shape[-1]
    dx = XATTN_HEADS * XATTN_HEAD_DIM
    ntok = bsz * seq
    mem2d = mem.reshape(bsz * nmem, dmodel)

    kv0 = _rms_matmul(mem2d, mem_norm_g[0], _bf(mem_kv_w[0]), tm=512, tn=2 * dx,
                      out_dtype=BF16).reshape(bsz, nmem, 2 * dx)
    proj0 = _rms_matmul(x.reshape(ntok, dmodel), norm_g[0], _bf(ml_w_in[0]), tm=1024, tn=2048,
                        out_dtype=BF16).reshape(bsz, seq, -1)
    wqk = jnp.concatenate([_blockdiag_tiles(ml_wq[0], MXU_DIM), _blockdiag_tiles(ml_wk[0], MXU_DIM)], axis=-1)
    wv = _blockdiag_tiles(ml_wv[0], MXU_DIM)
    wg = ml_w_gate[0].reshape(3, dmix, -1)
    xc, q, k, v0, gcol, grow = _mlstm_pre(
        proj0, ml_conv_w[0], ml_conv_b[0], _bf(wqk), _bf(wv),
        _bf(jnp.pad(wg, ((0, 0), (0, 0), (0, LANES - wg.shape[-1])))),
        _bf(jnp.pad(wg.transpose(0, 2, 1), ((0, 0), (0, 16 - wg.shape[-1]), (0, 0)))),
        ml_b_gate[0], ts=256)
    hs = _mlstm_scan(q, k, v0, gcol, grow, chunk=MLSTM_CHUNK)
    x1 = _mlstm_out(hs, xc, proj0, x, kv0, _bf(w_out[0]), ml_mhn_g[0], ml_skip[0], tm=256)

    w_in = rw_w_in[0]
    mu = rw_mu[0]
    cuts = [0, 3 * dmix]
    for rank in (DECAY_RANK, ICLR_RANK, VRES_RANK, GATE_RANK):
        cuts.append(cuts[-1] + rank)
    lora_w = jnp.concatenate([_pad_cols(w_in[:, cuts[i]:cuts[i + 1]], LORA_SEG) for i in range(1, 5)], axis=1)
    lora_mu = jnp.concatenate([jnp.pad(mu[cuts[i]:cuts[i + 1]], (0, LORA_SEG - (cuts[i + 1] - cuts[i])))
                               for i in range(1, 5)])
    w_packed = jnp.concatenate([w_in[:, :3 * dmix], w_in[:, cuts[-1]:], lora_w], axis=1)
    qm_block = (3 * dmix) // dx
    z_block = (3 * dmix + dx) // ((dmix + dx) // 2)
    lora_block = (3 * dmix + dx + dmix + dx) // (4 * LORA_SEG)

    kv1 = _rms_matmul(mem2d, mem_norm_g[1], _bf(mem_kv_w[1]), tm=512, tn=2 * dx,
                      out_dtype=BF16).reshape(bsz, nmem, 2 * dx)
    proj1 = _rms_matmul(x1.reshape(ntok, dmodel), norm_g[1], _bf(w_packed), tm=1024, tn=2560,
                        out_dtype=BF16).reshape(bsz, seq, -1)
    wl2 = _bf(jnp.stack([_pad_rows(rw_w_lora2[0], LORA_SEG), _pad_rows(rw_a_lora2[0], LORA_SEG),
                         _pad_rows(rw_v_lora2[0], LORA_SEG), _pad_rows(rw_g_lora2[0], LORA_SEG)]))
    vecs = jnp.stack([rw_w0[0], rw_a0[0], rw_v0[0], rw_k_k[0], rw_k_a[0], rw_r_k[0].reshape(-1),
                      jnp.zeros_like(rw_w0[0]), jnp.zeros_like(rw_w0[0])])
    lane = jnp.arange(LANES) // RWKV_HEAD_DIM
    e2 = _bf(lane[:, None] == lane[None, :])
    mus = (mu[:dmix].reshape(1, -1), mu[dmix:2 * dmix].reshape(1, -1), mu[2 * dmix:3 * dmix].reshape(1, -1),
           lora_mu.reshape(1, -1))
    r, lw, k2, v1, kk, ab, g, bonus = _rwkv_pre(proj1, v0, mus, wl2, vecs, e2, tm=256, lora_block=lora_block)
    o = _wkv7(r, lw, k2, v1, kk, ab, chunk=WKV_CHUNK)
    return _rwkv_out(o, g, bonus, proj1, x1, kv1, _bf(w_out[1]), rw_lnx_g[0], rw_lnx_b[0], final_g, e2,
                     tm=256, qm_block=qm_block, z_block=z_block)
```

```python
import functools
import math

import jax
import jax.numpy as jnp
from jax import lax
from jax.experimental import pallas as pl
from jax.experimental.pallas import tpu as pltpu

F32 = jnp.float32
BF16 = jnp.bfloat16

LANES = 128
MXU_DIM = 256
VMEM_LIMIT = 48 * 1024 * 1024

XATTN_HEADS = 4
XATTN_HEAD_DIM = 128
MLSTM_HEADS = 4
MLSTM_CONV = 4
QKV_BLOCK = 4
RWKV_HEAD_DIM = 64
DECAY_RANK, ICLR_RANK, VRES_RANK, GATE_RANK = 64, 64, 32, 128
LORA_SEG = 128

RMS_EPS = 1e-6
MHLN_EPS = 1e-5
RWKV_GN_EPS = 64e-5
L2_EPS = 1e-12

MLSTM_CHUNK = 256
WKV_CHUNK = 64
WKV_CHUNKS_PER_STEP = 4


def _params(*sem):
    return pltpu.CompilerParams(dimension_semantics=sem, vmem_limit_bytes=VMEM_LIMIT)


def _sigmoid(x):
    return 0.5 * jnp.tanh(0.5 * x) + 0.5


def _log_sigmoid(x):
    return jnp.minimum(x, 0.0) - jnp.log1p(jnp.exp(-jnp.abs(x)))


def _dot(a, b):
    return jnp.dot(a, b, preferred_element_type=F32)


def _dot_nt(a, b):
    return lax.dot_general(a, b, (((1,), (1,)), ((), ())), preferred_element_type=F32)


def _dot_tn(a, b):
    return lax.dot_general(a, b, (((0,), (0,)), ((), ())), preferred_element_type=F32)


def _bf(x):
    return x.astype(BF16)


def _f32(x):
    return x.astype(F32)


def _bf16_terms(x):
    hi = _bf(x)
    rest = x - _f32(hi)
    mid = _bf(rest)
    return hi, mid, _bf(rest - _f32(mid))


def _rms_matmul_body(x_ref, g_ref, w_ref, o_ref, h_ref):
    @pl.when(pl.program_id(1) == 0)
    def _():
        x = x_ref[...]
        ms = jnp.mean(x * x, axis=-1, keepdims=True)
        h_ref[...] = _bf(x * lax.rsqrt(ms + RMS_EPS) * g_ref[...])

    o_ref[...] = _dot(h_ref[...], w_ref[...]).astype(o_ref.dtype)


def _rms_matmul(x2d, g, w, *, tm, tn, out_dtype):
    m, d = x2d.shape
    n = w.shape[1]
    return pl.pallas_call(
        _rms_matmul_body,
        grid=(m // tm, n // tn),
        in_specs=[pl.BlockSpec((tm, d), lambda i, j: (i, 0)),
                  pl.BlockSpec((1, d), lambda i, j: (0, 0)),
                  pl.BlockSpec((d, tn), lambda i, j: (0, j))],
        out_specs=pl.BlockSpec((tm, tn), lambda i, j: (i, j)),
        out_shape=jax.ShapeDtypeStruct((m, n), out_dtype),
        scratch_shapes=[pltpu.VMEM((tm, d), BF16)],
        compiler_params=_params("parallel", "arbitrary"),
        name="rms_matmul",
    )(x2d, g.reshape(1, d), w)


def _mlstm_pre_body(u_ref, cw_ref, cb_ref, wqk_ref, wv_ref, wg_ref, wgt_ref, bgr_ref, bgc_ref,
                    xc_ref, q_ref, k_ref, v_ref, gcol_ref, grow_ref, ext_ref, *, ts, nblk):
    s = pl.program_id(1)

    @pl.when(s == 0)
    def _():
        ext_ref[0:8, :] = jnp.zeros((8, ext_ref.shape[1]), F32)

    @pl.when(s > 0)
    def _():
        ext_ref[0:8, :] = ext_ref[ts:ts + 8, :]

    ub = u_ref[...]
    u = _f32(ub)
    ext_ref[8:8 + ts, :] = u
    acc = cb_ref[...] + cw_ref[MLSTM_CONV - 1:MLSTM_CONV, :] * u
    for j in range(1, MLSTM_CONV):
        acc = acc + cw_ref[MLSTM_CONV - 1 - j:MLSTM_CONV - j, :] * ext_ref[8 - j:8 - j + ts, :]
    xc = acc * _sigmoid(acc)
    xcb = _bf(xc)
    xc_ref[...] = xcb

    for b in range(nblk):
        lo, hi = b * MXU_DIM, (b + 1) * MXU_DIM
        qk = _dot(xcb[:, lo:hi], wqk_ref[b])
        q_ref[:, lo:hi] = _bf(qk[:, :MXU_DIM])
        k_ref[:, lo:hi] = _bf(qk[:, MXU_DIM:])
        v_ref[:, lo:hi] = _bf(_dot(ub[:, lo:hi], wv_ref[b]))

    qb, kb, vb = q_ref[...], k_ref[...], v_ref[...]
    gc = _dot(qb, wg_ref[0]) + _dot(kb, wg_ref[1]) + _dot(vb, wg_ref[2]) + bgr_ref[...]
    lane = lax.broadcasted_iota(jnp.int32, gc.shape, 1)
    gc = jnp.where(lane < MLSTM_HEADS, gc, _log_sigmoid(gc))
    gcol_ref[...] = gc[:, :2 * MLSTM_HEADS]

    gr = _dot_nt(wgt_ref[0], qb) + _dot_nt(wgt_ref[1], kb) + _dot_nt(wgt_ref[2], vb)
    gr = gr[:2 * MLSTM_HEADS, :] + bgc_ref[...]
    row = lax.broadcasted_iota(jnp.int32, gr.shape, 0)
    grow_ref[...] = jnp.where(row < MLSTM_HEADS, gr, _log_sigmoid(gr))


def _mlstm_pre(proj, conv_w, conv_b, wqk, wv, wg, wgt, bg, *, ts):
    bsz, seq, _ = proj.shape
    dmix = conv_w.shape[1]
    nblk = dmix // MXU_DIM
    ng = 2 * MLSTM_HEADS
    full = lambda shape: pl.BlockSpec(shape, lambda b, s: (0,) * len(shape))
    tile = pl.BlockSpec((None, ts, dmix), lambda b, s: (b, s, 0))
    act = jax.ShapeDtypeStruct((bsz, seq, dmix), BF16)
    return pl.pallas_call(
        functools.partial(_mlstm_pre_body, ts=ts, nblk=nblk),
        grid=(bsz, seq // ts),
        in_specs=[tile, full(conv_w.shape), full((1, dmix)), full(wqk.shape), full(wv.shape),
                  full(wg.shape), full(wgt.shape), full((1, LANES)), full((ng, 1))],
        out_specs=[tile, tile, tile, tile,
                   pl.BlockSpec((None, ts, ng), lambda b, s: (b, s, 0)),
                   pl.BlockSpec((None, ng, ts), lambda b, s: (b, 0, s))],
        out_shape=[act, act, act, act,
                   jax.ShapeDtypeStruct((bsz, seq, ng), F32),
                   jax.ShapeDtypeStruct((bsz, ng, seq), F32)],
        scratch_shapes=[pltpu.VMEM((ts + 8, dmix), F32)],
        compiler_params=_params("parallel", "arbitrary"),
        name="mlstm_pre",
    )(proj, conv_w, conv_b.reshape(1, dmix), wqk, wv, wg, wgt,
      jnp.pad(bg, (0, LANES - ng)).reshape(1, LANES), bg.reshape(ng, 1))


def _mlstm_scan_body(q_ref, k_ref, v_ref, gcol_ref, grow_ref, h_ref, c_ref, n_ref, m_ref, *, chunk, dh):
    @pl.when(pl.program_id(1) == 0)
    def _():
        c_ref[...] = jnp.zeros(c_ref.shape, F32)
        n_ref[...] = jnp.zeros(n_ref.shape, F32)
        m_ref[...] = jnp.zeros(m_ref.shape, F32)

    gc = gcol_ref[...]
    gr = grow_ref[...]
    row = lax.broadcasted_iota(jnp.int32, (chunk, chunk), 0)
    col = lax.broadcasted_iota(jnp.int32, (chunk, chunk), 1)
    causal = col <= row
    tril = causal.astype(F32)
    bcol_all = sum(_dot(_bf(tril), t) for t in _bf16_terms(gc))
    brow_all = sum(_dot_nt(t, _bf(tril)) for t in _bf16_terms(gr))
    scale = dh ** -0.5
    for h in range(MLSTM_HEADS):
        lo, hi = h * dh, (h + 1) * dh
        i_col = gc[:, h:h + 1]
        b_col = bcol_all[:, MLSTM_HEADS + h:MLSTM_HEADS + h + 1]
        i_row = gr[h:h + 1, :]
        b_row = brow_all[MLSTM_HEADS + h:MLSTM_HEADS + h + 1, :]
        m_old = m_ref[h][0:1, 0:1]
        d_mat = jnp.where(causal, b_col - b_row + i_row, -jnp.inf)
        inter = b_col + m_old
        m_t = jnp.maximum(jnp.max(d_mat, axis=-1, keepdims=True), inter)
        qb = q_ref[:, lo:hi]
        qh = _f32(qb)
        kh = _f32(k_ref[:, lo:hi]) * scale
        kb, vb = _bf(kh), v_ref[:, lo:hi]
        s = _dot_nt(qb, kb) * jnp.exp(d_mat - m_t)
        g_in = jnp.exp(inter - m_t)
        c_old = c_ref[h]
        n_old = n_ref[h]
        num = _dot(_bf(s), vb) + g_in * _dot(qb, _bf(c_old))
        den = jnp.sum(s, axis=-1, keepdims=True) + g_in * jnp.sum(qh * n_old, axis=-1, keepdims=True)
        h_ref[:, lo:hi] = _bf(num / jnp.maximum(jnp.abs(den), jnp.exp(-m_t)))
        b_last = b_col[chunk - 1:chunk, :]
        w_s = b_last - b_col + i_col
        m_new = jnp.maximum(b_last + m_old, jnp.max(w_s, axis=0, keepdims=True))
        kw = kh * jnp.exp(w_s - m_new)
        g_old = jnp.exp(b_last + m_old - m_new)
        c_ref[h] = g_old * c_old + _dot_tn(_bf(kw), vb)
        n_ref[h] = g_old * n_old + jnp.sum(kw, axis=0, keepdims=True)
        m_ref[h] = jnp.broadcast_to(m_new, m_ref.shape[1:])


def _mlstm_scan(q, k, v, gcol, grow, *, chunk):
    bsz, seq, dmix = q.shape
    dh = dmix // MLSTM_HEADS
    ng = 2 * MLSTM_HEADS
    tile = pl.BlockSpec((None, chunk, dmix), lambda b, c: (b, c, 0))
    return pl.pallas_call(
        functools.partial(_mlstm_scan_body, chunk=chunk, dh=dh),
        grid=(bsz, seq // chunk),
        in_specs=[tile, tile, tile,
                  pl.BlockSpec((None, chunk, ng), lambda b, c: (b, c, 0)),
                  pl.BlockSpec((None, ng, chunk), lambda b, c: (b, 0, c))],
        out_specs=tile,
        out_shape=jax.ShapeDtypeStruct((bsz, seq, dmix), BF16),
        scratch_shapes=[pltpu.VMEM((MLSTM_HEADS, dh, dh), F32),
                        pltpu.VMEM((MLSTM_HEADS, 1, dh), F32),
                        pltpu.VMEM((MLSTM_HEADS, 8, LANES), F32)],
        compiler_params=_params("parallel", "arbitrary"),
        name="mlstm_scan",
    )(q, k, v, gcol, grow)


def _silu(z):
    return z * _sigmoid(z)


def _memory_attention_into(y_ref, qm, kv_ref, z_attn, col0):
    dx = XATTN_HEADS * XATTN_HEAD_DIM
    for h in range(XATTN_HEADS):
        lo, hi = h * XATTN_HEAD_DIM, (h + 1) * XATTN_HEAD_DIM
        s = _dot_nt(qm[:, lo:hi], kv_ref[:, lo:hi]) * (XATTN_HEAD_DIM ** -0.5)
        p = jnp.exp(s - jnp.max(s, axis=-1, keepdims=True))
        o = _dot(_bf(p), kv_ref[:, dx + lo:dx + hi]) / jnp.sum(p, axis=-1, keepdims=True)
        y_ref[:, col0 + lo:col0 + hi] = _bf(o * _silu(_f32(z_attn[:, lo:hi])))


def _mlstm_out_body(hs_ref, xc_ref, qm_ref, z_ref, x_ref, kv_ref, wo_ref, mg_ref, sk_ref,
                    o_ref, y_ref, *, dh):
    dmix = MLSTM_HEADS * dh
    for h in range(MLSTM_HEADS):
        lo, hi = h * dh, (h + 1) * dh
        seg = _f32(hs_ref[:, lo:hi])
        d = seg - jnp.mean(seg, axis=-1, keepdims=True)
        var = jnp.mean(d * d, axis=-1, keepdims=True)
        ymix = d * lax.rsqrt(var + MHLN_EPS) * mg_ref[:, lo:hi] + sk_ref[:, lo:hi] * _f32(xc_ref[:, lo:hi])
        y_ref[:, lo:hi] = _bf(ymix * _silu(_f32(z_ref[:, lo:hi])))
    _memory_attention_into(y_ref, qm_ref[...], kv_ref, z_ref[:, dmix:], dmix)
    o_ref[...] = x_ref[...] + _dot(y_ref[...], wo_ref[...])


def _mlstm_out(hs, xc, proj, x, kv, w_out, mhn_g, skip, *, tm):
    bsz, seq, dmix = hs.shape
    dmodel = x.shape[-1]
    dx = XATTN_HEADS * XATTN_HEAD_DIM
    dinner = dmix + dx
    nmem = kv.shape[1]
    full = lambda shape: pl.BlockSpec(shape, lambda b, s: (0,) * len(shape))
    tile = lambda w, j: pl.BlockSpec((None, tm, w), lambda b, s: (b, s, j))
    return pl.pallas_call(
        functools.partial(_mlstm_out_body, dh=dmix // MLSTM_HEADS),
        grid=(bsz, seq // tm),
        in_specs=[tile(dmix, 0), tile(dmix, 0),
                  tile(dx, dmix // dx), tile(dinner, 1), tile(dmodel, 0),
                  pl.BlockSpec((None, nmem, 2 * dx), lambda b, s: (b, 0, 0)),
                  full(w_out.shape), full((1, dmix)), full((1, dmix))],
        out_specs=tile(dmodel, 0),
        out_shape=jax.ShapeDtypeStruct((bsz, seq, dmodel), F32),
        scratch_shapes=[pltpu.VMEM((tm, dinner), BF16)],
        compiler_params=_params("parallel", "parallel"),
        name="mlstm_out",
    )(hs, xc, proj, proj, x, kv, w_out, mhn_g.reshape(1, dmix), skip.reshape(1, dmix))


def _head_dot(x, e):
    return _dot(_bf(x), e)


def _head_sums(x, e_ref):
    tm = x.shape[0]
    nslab = x.shape[1] // LANES
    rows = jnp.concatenate([x[:, i * LANES:(i + 1) * LANES] for i in range(nslab)], axis=0)
    sums = _head_dot(rows, e_ref[...])
    return jnp.concatenate([sums[i * tm:(i + 1) * tm] for i in range(nslab)], axis=1)


def _token_shift(x, carry_ref, mu):
    tm = x.shape[0]
    prev = pltpu.roll(x, 1, axis=0)
    first = lax.broadcasted_iota(jnp.int32, (tm, 1), 0) == 0
    prev = jnp.where(first, carry_ref[7:8, :], prev)
    carry_ref[...] = x[tm - 8:tm, :]
    return x + (prev - x) * mu


def _rwkv_pre_body(pr_ref, pk_ref, pv_ref, pl_ref, vf_ref, mur_ref, muk_ref, muv_ref, mul_ref,
                   wl2_ref, vecs_ref, e_ref,
                   r_ref, lw_ref, k_ref, v_ref, kk_ref, ab_ref, g_ref, bonus_ref,
                   cr_ref, ck_ref, cv_ref, cl_ref):
    @pl.when(pl.program_id(1) == 0)
    def _():
        for c in (cr_ref, ck_ref, cv_ref, cl_ref):
            c[...] = jnp.zeros(c.shape, F32)

    r = _token_shift(_f32(pr_ref[...]), cr_ref, mur_ref[...])
    k = _token_shift(_f32(pk_ref[...]), ck_ref, muk_ref[...])
    v = _token_shift(_f32(pv_ref[...]), cv_ref, muv_ref[...])
    lo = _token_shift(_f32(pl_ref[...]), cl_ref, mul_ref[...])
    w0, a0, v0, k_k, k_a, r_k = (vecs_ref[i:i + 1, :] for i in range(6))

    seg = lambda i: lo[:, i * LORA_SEG:(i + 1) * LORA_SEG]
    d = w0 + _dot(_bf(jnp.tanh(seg(0))), wl2_ref[0])
    lw_ref[...] = -math.exp(-0.5) * _sigmoid(d)
    a = _sigmoid(a0 + _dot(_bf(seg(1)), wl2_ref[1]))
    v = v + (_f32(vf_ref[...]) - v) * _sigmoid(v0 + _dot(_bf(seg(2)), wl2_ref[2]))
    g_ref[...] = _bf(_dot(_bf(_sigmoid(seg(3))), wl2_ref[3]))

    kk = k * k_k
    kk = kk * lax.rsqrt(jnp.maximum(_head_sums(kk * kk, e_ref), L2_EPS * L2_EPS))
    k2 = k * (1.0 + (a - 1.0) * k_a)
    r_ref[...] = _bf(r)
    k_ref[...] = _bf(k2)
    v_ref[...] = _bf(v)
    kk_ref[...] = _bf(kk)
    ab_ref[...] = _bf(kk * a)
    bonus_ref[...] = _bf(_head_sums(r * k2 * r_k, e_ref) * v)


def _rwkv_pre(proj, v_first, mus, wl2, vecs, e2, *, tm, lora_block):
    bsz, seq, dmix = v_first.shape
    lw = 4 * LORA_SEG
    full = lambda shape: pl.BlockSpec(shape, lambda b, s: (0,) * len(shape))
    tile = lambda w, j: pl.BlockSpec((None, tm, w), lambda b, s: (b, s, j))
    act = jax.ShapeDtypeStruct((bsz, seq, dmix), BF16)
    log_decay = jax.ShapeDtypeStruct((bsz, seq, dmix), F32)
    return pl.pallas_call(
        _rwkv_pre_body,
        grid=(bsz, seq // tm),
        in_specs=[tile(dmix, 0), tile(dmix, 1), tile(dmix, 2), tile(lw, lora_block), tile(dmix, 0),
                  full((1, dmix)), full((1, dmix)), full((1, dmix)), full((1, lw)),
                  full(wl2.shape), full(vecs.shape), full(e2.shape)],
        out_specs=[tile(dmix, 0)] * 8,
        out_shape=[act, log_decay] + [act] * 6,
        scratch_shapes=[pltpu.VMEM((8, dmix), F32)] * 3 + [pltpu.VMEM((8, lw), F32)],
        compiler_params=_params("parallel", "arbitrary"),
        name="rwkv_pre",
    )(proj, proj, proj, proj, v_first, *mus, wl2, vecs, e2)


def _wkv7_body(r_ref, lw_ref, k_ref, v_ref, kk_ref, ab_ref, o_ref, h_ref, *, chunk, npair, nsub):
    @pl.when(pl.program_id(1) == 0)
    def _():
        h_ref[...] = jnp.zeros(h_ref.shape, F32)

    c2 = 2 * chunk
    row = lax.broadcasted_iota(jnp.int32, (chunk, chunk), 0)
    col = lax.broadcasted_iota(jnp.int32, (chunk, chunk), 1)
    tril = (col <= row).astype(F32)
    def scaled_operands(rows, cols):
        lw = lw_ref[rows, cols]
        g_inc = sum(_dot(_bf(tril), t) for t in _bf16_terms(lw))
        g_last = g_inc[chunk - 1:chunk, :]
        e_neg = jnp.exp(-g_inc)
        e_end = jnp.exp(g_last - g_inc)
        kk = _f32(kk_ref[rows, cols])
        ab = _f32(ab_ref[rows, cols])
        kx = _f32(k_ref[rows, cols])
        return (-kk * jnp.exp(g_inc - lw),
                ab * e_neg,
                kx * e_neg,
                _f32(r_ref[rows, cols]) * jnp.exp(g_inc),
                ab * e_end,
                kx * e_end,
                jnp.exp(g_last))

    lane_head = lax.broadcasted_iota(jnp.int32, (c2, LANES), 1) // RWKV_HEAD_DIM
    row_head = lax.broadcasted_iota(jnp.int32, (c2, LANES), 0) // chunk
    head_mask = lane_head == row_head
    ri = lax.broadcasted_iota(jnp.int32, (c2, c2), 0)
    ci = lax.broadcasted_iota(jnp.int32, (c2, c2), 1)
    same = (ri // chunk) == (ci // chunk)
    tp = lax.broadcasted_iota(jnp.int32, (chunk, c2), 0)
    sp = lax.broadcasted_iota(jnp.int32, (chunk, c2), 1) % chunk
    strict = sp < tp
    incl = sp <= tp
    eye_p = (sp == tp).astype(F32)
    li = lax.broadcasted_iota(jnp.int32, (LANES, LANES), 0)
    lj = lax.broadcasted_iota(jnp.int32, (LANES, LANES), 1)
    lane_eye = (li == lj).astype(F32)
    lane_same = (li // RWKV_HEAD_DIM) == (lj // RWKV_HEAD_DIM)

    def stack(x):
        return _bf(jnp.where(head_mask, jnp.concatenate([x, x], axis=0), 0.0))

    def blocks(x):
        return _bf(jnp.where(same, jnp.concatenate([x, x], axis=0), 0.0))

    def cat0(*xs):
        return jnp.concatenate(xs, axis=0)

    def cat1(*xs):
        return jnp.concatenate(xs, axis=1)

    each = lambda f, *xs: [f(*t) for t in zip(*xs)]
    nlevel = int(math.log2(chunk))
    sls = [slice(p * LANES, (p + 1) * LANES) for p in range(npair)]
    rows_of = lambda s: slice(s * chunk, (s + 1) * chunk)
    operands = [scaled_operands(rows_of(0), sl) for sl in sls]
    for sub in range(nsub):
        rows = rows_of(sub)
        a_t, b_t, k_t, r_t, b_e, k_e, gam_l = map(list, zip(*operands))
        operands = []
        todo = list(sls) if sub + 1 < nsub else []

        def feed(n=1):
            for _ in range(min(n, len(todo))):
                operands.append(scaled_operands(rows_of(sub + 1), todo.pop(0)))

        v_s = [stack(_f32(v_ref[rows, sl])) for sl in sls]
        sc = each(lambda a, r, b, k: _dot_nt(_bf(cat0(a, r)), cat0(stack(b), stack(k))), a_t, r_t, b_t, k_t)
        feed()
        d_ab = [jnp.where(strict, s[:chunk, :c2], 0.0) for s in sc]
        d_ak = [_bf(jnp.where(strict, s[:chunk, c2:], 0.0)) for s in sc]
        d_rb = [_bf(jnp.where(incl, s[chunk:, :c2], 0.0)) for s in sc]
        d_rk = [_bf(jnp.where(incl, s[chunk:, c2:], 0.0)) for s in sc]
        t_inv = [eye_p + d for d in d_ab]
        d_pow = d_ab
        for level in range(nlevel):
            rhs = each(blocks, d_pow)
            if level == 0:
                d_pow = each(lambda d, w: _dot(_bf(d), w), d_pow, rhs)
            elif level < nlevel - 1:
                both = each(lambda d, t, w: _dot(_bf(cat0(d, t)), w), d_pow, t_inv, rhs)
                d_pow = [x[:chunk] for x in both]
                t_inv = each(lambda t, x: t + x[chunk:], t_inv, both)
            else:
                t_inv = each(lambda t, w: t + _dot(_bf(t), w), t_inv, rhs)
            feed()
        akv = each(_dot, d_ak, v_s)
        feed()
        wu = each(lambda t, a, x: _dot(_bf(t), cat1(stack(a), stack(x))), t_inv, a_t, akv)
        feed()
        ry = each(lambda drb, x: _dot(drb, cat1(stack(x[:, :LANES]), stack(x[:, LANES:]))), d_rb, wu)
        feed()
        y0 = each(lambda x, drk, v: x[:, LANES:] + _dot(drk, v), ry, d_rk, v_s)
        rp = each(lambda r, x: r + x[:, :LANES], r_t, ry)
        feed()
        pq = each(lambda be, x: _dot_tn(_bf(be), _bf(x)), b_e, wu)
        feed()
        kv = [_dot_tn(_bf(ke), v_ref[rows, sl]) for ke, sl in zip(k_e, sls)]
        feed(len(todo))
        p_mat = each(lambda g, x: lane_eye * g + jnp.where(lane_same, x[:, :LANES], 0.0), gam_l, pq)
        q_mat = each(lambda x, y: jnp.where(lane_same, x[:, LANES:] + y, 0.0), pq, kv)
        for i, sl in enumerate(sls):
            yh = _dot(_bf(cat0(rp[i], p_mat[i])), _bf(h_ref[i]))
            o_ref[rows, sl] = _bf(yh[:chunk] + y0[i])
            h_ref[i] = yh[chunk:] + q_mat[i]


def _wkv7(r, lw, k, v, kk, ab, *, chunk):
    bsz, seq, dmix = r.shape
    npair = dmix // LANES
    nsub = WKV_CHUNKS_PER_STEP
    tile = pl.BlockSpec((None, nsub * chunk, dmix), lambda b, c: (b, c, 0))
    return pl.pallas_call(
        functools.partial(_wkv7_body, chunk=chunk, npair=npair, nsub=nsub),
        grid=(bsz, seq // (nsub * chunk)),
        in_specs=[tile] * 6,
        out_specs=tile,
        out_shape=jax.ShapeDtypeStruct((bsz, seq, dmix), BF16),
        scratch_shapes=[pltpu.VMEM((npair, LANES, LANES), F32)],
        compiler_params=_params("parallel", "arbitrary"),
        name="wkv7",
    )(r, lw, k, v, kk, ab)


def _rwkv_out_body(o_ref, g_ref, bonus_ref, qm_ref, za_ref, zb_ref, x_ref, kv_ref, wo_ref,
                   lg_ref, lb_ref, fg_ref, e_ref, out_ref, y_ref, *, dmix):
    half = za_ref.shape[1]
    inv_n = 1.0 / RWKV_HEAD_DIM
    for i in range(dmix // LANES):
        sl = slice(i * LANES, (i + 1) * LANES)
        seg = _f32(o_ref[:, sl])
        d = seg - _head_dot(seg, e_ref[...]) * inv_n
        var = _head_dot(d * d, e_ref[...]) * inv_n
        ymix = ((d * lax.rsqrt(var + RWKV_GN_EPS) * lg_ref[:, sl] + lb_ref[:, sl] + _f32(bonus_ref[:, sl]))
                * _f32(g_ref[:, sl]))
        z = za_ref[:, sl] if (i + 1) * LANES <= half else zb_ref[:, i * LANES - half:(i + 1) * LANES - half]
        y_ref[:, sl] = _bf(ymix * _silu(_f32(z)))
    _memory_attention_into(y_ref, qm_ref[...], kv_ref, zb_ref[:, dmix - half:], dmix)
    xn = x_ref[...] + _dot(y_ref[...], wo_ref[...])
    ms = jnp.mean(xn * xn, axis=-1, keepdims=True)
    out_ref[...] = xn * lax.rsqrt(ms + RMS_EPS) * fg_ref[...]


def _rwkv_out(o, g, bonus, proj, x, kv, w_out, lnx_g, lnx_b, final_g, e2, *, tm, qm_block, z_block):
    bsz, seq, dmix = o.shape
    dmodel = x.shape[-1]
    dx = XATTN_HEADS * XATTN_HEAD_DIM
    dinner = dmix + dx
    half = dinner // 2
    nmem = kv.shape[1]
    full = lambda shape: pl.BlockSpec(shape, lambda b, s: (0,) * len(shape))
    tile = lambda w, j: pl.BlockSpec((None, tm, w), lambda b, s: (b, s, j))
    return pl.pallas_call(
        functools.partial(_rwkv_out_body, dmix=dmix),
        grid=(bsz, seq // tm),
        in_specs=[tile(dmix, 0), tile(dmix, 0), tile(dmix, 0),
                  tile(dx, qm_block), tile(half, z_block), tile(half, z_block + 1), tile(dmodel, 0),
                  pl.BlockSpec((None, nmem, 2 * dx), lambda b, s: (b, 0, 0)),
                  full(w_out.shape), full((1, dmix)), full((1, dmix)), full((1, dmodel)), full(e2.shape)],
        out_specs=tile(dmodel, 0),
        out_shape=jax.ShapeDtypeStruct((bsz, seq, dmodel), F32),
        scratch_shapes=[pltpu.VMEM((tm, dinner), BF16)],
        compiler_params=_params("parallel", "parallel"),
        name="rwkv_out",
    )(o, g, bonus, proj, proj, proj, x, kv, w_out, lnx_g.reshape(1, dmix), lnx_b.reshape(1, dmix),
      final_g.reshape(1, dmodel), e2)


def _blockdiag_tiles(w, tile):
    nb, blk, _ = w.shape
    rows = w.reshape(nb * blk // tile, tile, blk)
    idx = jnp.arange(tile)
    on_diag = (idx[:, None] // blk) == (idx[None, :] // blk)
    return jnp.where(on_diag, jnp.tile(rows, (1, 1, tile // blk)), 0.0)


def _pad_rows(w, rows):
    return jnp.pad(w, ((0, rows - w.shape[0]), (0, 0)))


def _pad_cols(w, cols):
    return jnp.pad(w, ((0, 0), (0, cols - w.shape[1])))


def kernel(x, mem, norm_g, mem_norm_g, mem_kv_w, w_out, ml_w_in, ml_conv_w, ml_conv_b, ml_wq, ml_wk, ml_wv, ml_w_gate, ml_b_gate, ml_mhn_g, ml_skip, rw_w_in, rw_mu, rw_w_lora2, rw_w0, rw_a_lora2, rw_a0, rw_v_lora2, rw_v0, rw_g_lora2, rw_k_k, rw_k_a, rw_r_k, rw_lnx_g, rw_lnx_b, final_g):
    bsz, seq, dmodel = x.shape
    nmem = mem.shape[1]
    dmix = ml_conv_w.shape[-1]
    dx = XATTN_HEADS * XATTN_HEAD_DIM
    ntok = bsz * seq
    mem2d = mem.reshape(bsz * nmem, dmodel)

    kv0 = _rms_matmul(mem2d, mem_norm_g[0], _bf(mem_kv_w[0]), tm=512, tn=2 * dx,
                      out_dtype=BF16).reshape(bsz, nmem, 2 * dx)
    proj0 = _rms_matmul(x.reshape(ntok, dmodel), norm_g[0], _bf(ml_w_in[0]), tm=1024, tn=2048,
                        out_dtype=BF16).reshape(bsz, seq, -1)
    wqk = jnp.concatenate([_blockdiag_tiles(ml_wq[0], MXU_DIM), _blockdiag_tiles(ml_wk[0], MXU_DIM)], axis=-1)
    wv = _blockdiag_tiles(ml_wv[0], MXU_DIM)
    wg = ml_w_gate[0].reshape(3, dmix, -1)
    xc, q, k, v0, gcol, grow = _mlstm_pre(
        proj0, ml_conv_w[0], ml_conv_b[0], _bf(wqk), _bf(wv),
        _bf(jnp.pad(wg, ((0, 0), (0, 0), (0, LANES - wg.shape[-1])))),
        _bf(jnp.pad(wg.transpose(0, 2, 1), ((0, 0), (0, 16 - wg.shape[-1]), (0, 0)))),
        ml_b_gate[0], ts=256)
    hs = _mlstm_scan(q, k, v0, gcol, grow, chunk=MLSTM_CHUNK)
    x1 = _mlstm_out(hs, xc, proj0, x, kv0, _bf(w_out[0]), ml_mhn_g[0], ml_skip[0], tm=256)

    w_in = rw_w_in[0]
    mu = rw_mu[0]
    cuts = [0, 3 * dmix]
    for rank in (DECAY_RANK, ICLR_RANK, VRES_RANK, GATE_RANK):
        cuts.append(cuts[-1] + rank)
    lora_w = jnp.concatenate([_pad_cols(w_in[:, cuts[i]:cuts[i + 1]], LORA_SEG) for i in range(1, 5)], axis=1)
    lora_mu = jnp.concatenate([jnp.pad(mu[cuts[i]:cuts[i + 1]], (0, LORA_SEG - (cuts[i + 1] - cuts[i])))
                               for i in range(1, 5)])
    w_packed = jnp.concatenate([w_in[:, :3 * dmix], w_in[:, cuts[-1]:], lora_w], axis=1)
    qm_block = (3 * dmix) // dx
    z_block = (3 * dmix + dx) // ((dmix + dx) // 2)
    lora_block = (3 * dmix + dx + dmix + dx) // (4 * LORA_SEG)

    kv1 = _rms_matmul(mem2d, mem_norm_g[1], _bf(mem_kv_w[1]), tm=512, tn=2 * dx,
                      out_dtype=BF16).reshape(bsz, nmem, 2 * dx)
    proj1 = _rms_matmul(x1.reshape(ntok, dmodel), norm_g[1], _bf(w_packed), tm=1024, tn=2560,
                        out_dtype=BF16).reshape(bsz, seq, -1)
    wl2 = _bf(jnp.stack([_pad_rows(rw_w_lora2[0], LORA_SEG), _pad_rows(rw_a_lora2[0], LORA_SEG),
                         _pad_rows(rw_v_lora2[0], LORA_SEG), _pad_rows(rw_g_lora2[0], LORA_SEG)]))
    vecs = jnp.stack([rw_w0[0], rw_a0[0], rw_v0[0], rw_k_k[0], rw_k_a[0], rw_r_k[0].reshape(-1),
                      jnp.zeros_like(rw_w0[0]), jnp.zeros_like(rw_w0[0])])
    lane = jnp.arange(LANES) // RWKV_HEAD_DIM
    e2 = _bf(lane[:, None] == lane[None, :])
    mus = (mu[:dmix].reshape(1, -1), mu[dmix:2 * dmix].reshape(1, -1), mu[2 * dmix:3 * dmix].reshape(1, -1),
           lora_mu.reshape(1, -1))
    r, lw, k2, v1, kk, ab, g, bonus = _rwkv_pre(proj1, v0, mus, wl2, vecs, e2, tm=256, lora_block=lora_block)
    o = _wkv7(r, lw, k2, v1, kk, ab, chunk=WKV_CHUNK)
    return _rwkv_out(o, g, bonus, proj1, x1, kv1, _bf(w_out[1]), rw_lnx_g[0], rw_lnx_b[0], final_g, e2,
                     tm=256, qm_block=qm_block, z_block=z_block)
```

```python
import functools
import math

import jax
import jax.numpy as jnp
from jax import lax
from jax.experimental import pallas as pl
from jax.experimental.pallas import tpu as pltpu

F32 = jnp.float32
BF16 = jnp.bfloat16

LANES = 128
MXU_DIM = 256
VMEM_LIMIT = 48 * 1024 * 1024

XATTN_HEADS = 4
XATTN_HEAD_DIM = 128
MLSTM_HEADS = 4
MLSTM_CONV = 4
QKV_BLOCK = 4
RWKV_HEAD_DIM = 64
DECAY_RANK, ICLR_RANK, VRES_RANK, GATE_RANK = 64, 64, 32, 128
LORA_SEG = 128

RMS_EPS = 1e-6
MHLN_EPS = 1e-5
RWKV_GN_EPS = 64e-5
L2_EPS = 1e-12

MLSTM_CHUNK = 256
WKV_CHUNK = 64
WKV_CHUNKS_PER_STEP = 4
OUT_SUB_ROWS = 256


def _params(*sem):
    return pltpu.CompilerParams(dimension_semantics=sem, vmem_limit_bytes=VMEM_LIMIT)


def _sigmoid(x):
    return 0.5 * jnp.tanh(0.5 * x) + 0.5


def _log_sigmoid(x):
    return jnp.minimum(x, 0.0) - jnp.log1p(jnp.exp(-jnp.abs(x)))


def _dot(a, b):
    return jnp.dot(a, b, preferred_element_type=F32)


def _dot_nt(a, b):
    return lax.dot_general(a, b, (((1,), (1,)), ((), ())), preferred_element_type=F32)


def _dot_tn(a, b):
    return lax.dot_general(a, b, (((0,), (0,)), ((), ())), preferred_element_type=F32)


def _bf(x):
    return x.astype(BF16)


def _f32(x):
    return x.astype(F32)


def _bf16_terms(x):
    hi = _bf(x)
    rest = x - _f32(hi)
    mid = _bf(rest)
    return hi, mid, _bf(rest - _f32(mid))


def _rms_matmul_body(x_ref, g_ref, w_ref, o_ref, h_ref):
    @pl.when(pl.program_id(1) == 0)
    def _():
        x = x_ref[...]
        ms = jnp.mean(x * x, axis=-1, keepdims=True)
        h_ref[...] = _bf(x * lax.rsqrt(ms + RMS_EPS) * g_ref[...])

    o_ref[...] = _dot(h_ref[...], w_ref[...]).astype(o_ref.dtype)


def _rms_matmul(x2d, g, w, *, tm, tn, out_dtype):
    m, d = x2d.shape
    n = w.shape[1]
    return pl.pallas_call(
        _rms_matmul_body,
        grid=(m // tm, n // tn),
        in_specs=[pl.BlockSpec((tm, d), lambda i, j: (i, 0)),
                  pl.BlockSpec((1, d), lambda i, j: (0, 0)),
                  pl.BlockSpec((d, tn), lambda i, j: (0, j))],
        out_specs=pl.BlockSpec((tm, tn), lambda i, j: (i, j)),
        out_shape=jax.ShapeDtypeStruct((m, n), out_dtype),
        scratch_shapes=[pltpu.VMEM((tm, d), BF16)],
        compiler_params=_params("parallel", "arbitrary"),
        name="rms_matmul",
    )(x2d, g.reshape(1, d), w)


def _mlstm_pre_body(u_ref, cw_ref, cb_ref, wqk_ref, wv_ref, wg_ref, wgt_ref, bgr_ref, bgc_ref,
                    xc_ref, q_ref, k_ref, v_ref, gcol_ref, grow_ref, ext_ref, *, ts, nblk):
    s = pl.program_id(1)

    @pl.when(s == 0)
    def _():
        ext_ref[0:8, :] = jnp.zeros((8, ext_ref.shape[1]), F32)

    @pl.when(s > 0)
    def _():
        ext_ref[0:8, :] = ext_ref[ts:ts + 8, :]

    ext_ref[8:8 + ts, :] = _f32(u_ref[...])
    gc = bgr_ref[...]
    gr = jnp.zeros((wgt_ref.shape[1], ts), F32)
    for b in range(nblk):
        lo, hi = b * MXU_DIM, (b + 1) * MXU_DIM
        ub = u_ref[:, lo:hi]
        acc = cb_ref[:, lo:hi] + cw_ref[MLSTM_CONV - 1:MLSTM_CONV, lo:hi] * _f32(ub)
        for j in range(1, MLSTM_CONV):
            acc = acc + cw_ref[MLSTM_CONV - 1 - j:MLSTM_CONV - j, lo:hi] * ext_ref[8 - j:8 - j + ts, lo:hi]
        xcb = _bf(acc * _sigmoid(acc))
        xc_ref[:, lo:hi] = xcb
        qk = _dot(xcb, wqk_ref[b])
        qb, kb, vb = _bf(qk[:, :MXU_DIM]), _bf(qk[:, MXU_DIM:]), _bf(_dot(ub, wv_ref[b]))
        q_ref[:, lo:hi] = qb
        k_ref[:, lo:hi] = kb
        v_ref[:, lo:hi] = vb
        gc = gc + _dot(qb, wg_ref[0, lo:hi, :]) + _dot(kb, wg_ref[1, lo:hi, :]) + _dot(vb, wg_ref[2, lo:hi, :])
        gr = (gr + _dot_nt(wgt_ref[0, :, lo:hi], qb) + _dot_nt(wgt_ref[1, :, lo:hi], kb)
              + _dot_nt(wgt_ref[2, :, lo:hi], vb))
    lane = lax.broadcasted_iota(jnp.int32, gc.shape, 1)
    gc = jnp.where(lane < MLSTM_HEADS, gc, _log_sigmoid(gc))
    gcol_ref[...] = gc[:, :2 * MLSTM_HEADS]

    gr = gr[:2 * MLSTM_HEADS, :] + bgc_ref[...]
    row = lax.broadcasted_iota(jnp.int32, gr.shape, 0)
    grow_ref[...] = jnp.where(row < MLSTM_HEADS, gr, _log_sigmoid(gr))


def _mlstm_pre(proj, conv_w, conv_b, wqk, wv, wg, wgt, bg, *, ts):
    bsz, seq, _ = proj.shape
    dmix = conv_w.shape[1]
    nblk = dmix // MXU_DIM
    ng = 2 * MLSTM_HEADS
    full = lambda shape: pl.BlockSpec(shape, lambda b, s: (0,) * len(shape))
    tile = pl.BlockSpec((None, ts, dmix), lambda b, s: (b, s, 0))
    act = jax.ShapeDtypeStruct((bsz, seq, dmix), BF16)
    return pl.pallas_call(
        functools.partial(_mlstm_pre_body, ts=ts, nblk=nblk),
        grid=(bsz, seq // ts),
        in_specs=[tile, full(conv_w.shape), full((1, dmix)), full(wqk.shape), full(wv.shape),
                  full(wg.shape), full(wgt.shape), full((1, LANES)), full((ng, 1))],
        out_specs=[tile, tile, tile, tile,
                   pl.BlockSpec((None, ts, ng), lambda b, s: (b, s, 0)),
                   pl.BlockSpec((None, ng, ts), lambda b, s: (b, 0, s))],
        out_shape=[act, act, act, act,
                   jax.ShapeDtypeStruct((bsz, seq, ng), F32),
                   jax.ShapeDtypeStruct((bsz, ng, seq), F32)],
        scratch_shapes=[pltpu.VMEM((ts + 8, dmix), F32)],
        compiler_params=_params("parallel", "arbitrary"),
        name="mlstm_pre",
    )(proj, conv_w, conv_b.reshape(1, dmix), wqk, wv, wg, wgt,
      jnp.pad(bg, (0, LANES - ng)).reshape(1, LANES), bg.reshape(ng, 1))


def _mlstm_scan_body(q_ref, k_ref, v_ref, gcol_ref, grow_ref, h_ref, c_ref, n_ref, m_ref, *, chunk, dh):
    @pl.when(pl.program_id(1) == 0)
    def _():
        c_ref[...] = jnp.zeros(c_ref.shape, F32)
        n_ref[...] = jnp.zeros(n_ref.shape, F32)
        m_ref[...] = jnp.zeros(m_ref.shape, F32)

    gc = gcol_ref[...]
    gr = grow_ref[...]
    row = lax.broadcasted_iota(jnp.int32, (chunk, chunk), 0)
    col = lax.broadcasted_iota(jnp.int32, (chunk, chunk), 1)
    causal = col <= row
    tril = causal.astype(F32)
    bcol_all = sum(_dot(_bf(tril), t) for t in _bf16_terms(gc))
    brow_all = sum(_dot_nt(t, _bf(tril)) for t in _bf16_terms(gr))
    scale = dh ** -0.5
    cols = lambda h: slice(h * dh, (h + 1) * dh)
    st = [dict() for _ in range(MLSTM_HEADS)]

    def products(h):
        t = st[h]
        t['kh'] = _f32(k_ref[:, cols(h)]) * scale
        t['qk'] = _dot_nt(q_ref[:, cols(h)], _bf(t['kh']))
        t['qc'] = _dot(q_ref[:, cols(h)], _bf(c_ref[h]))

    def gates(h):
        t = st[h]
        i_col = gc[:, h:h + 1]
        b_col = bcol_all[:, MLSTM_HEADS + h:MLSTM_HEADS + h + 1]
        i_row = gr[h:h + 1, :]
        b_row = brow_all[MLSTM_HEADS + h:MLSTM_HEADS + h + 1, :]
        m_old = m_ref[h][0:1, 0:1]
        d_mat = jnp.where(causal, b_col - b_row + i_row, -jnp.inf)
        inter = b_col + m_old
        m_t = jnp.maximum(jnp.max(d_mat, axis=-1, keepdims=True), inter)
        t['e'] = jnp.exp(d_mat - m_t)
        t['g_in'] = jnp.exp(inter - m_t)
        t['floor'] = jnp.exp(-m_t)
        b_last = b_col[chunk - 1:chunk, :]
        w_s = b_last - b_col + i_col
        t['m_new'] = jnp.maximum(b_last + m_old, jnp.max(w_s, axis=0, keepdims=True))
        t['k_scale'] = jnp.exp(w_s - t['m_new'])
        t['g_old'] = jnp.exp(b_last + m_old - t['m_new'])

    def outputs(h):
        t = st[h]
        s = t['qk'] * t['e']
        num = _dot(_bf(s), v_ref[:, cols(h)]) + t['g_in'] * t['qc']
        den = (jnp.sum(s, axis=-1, keepdims=True)
               + t['g_in'] * jnp.sum(_f32(q_ref[:, cols(h)]) * n_ref[h], axis=-1, keepdims=True))
        h_ref[:, cols(h)] = _bf(num / jnp.maximum(jnp.abs(den), t['floor']))

    def state_update(h):
        t = st[h]
        kw = t['kh'] * t['k_scale']
        c_ref[h] = t['g_old'] * c_ref[h] + _dot_tn(_bf(kw), v_ref[:, cols(h)])
        n_ref[h] = t['g_old'] * n_ref[h] + jnp.sum(kw, axis=0, keepdims=True)
        m_ref[h] = jnp.broadcast_to(t['m_new'], m_ref.shape[1:])

    stages = (products, gates, outputs, state_update)
    for step in range(MLSTM_HEADS + len(stages) - 1):
        for depth, stage in enumerate(stages):
            if 0 <= step - depth < MLSTM_HEADS:
                stage(step - depth)


def _mlstm_scan(q, k, v, gcol, grow, *, chunk):
    bsz, seq, dmix = q.shape
    dh = dmix // MLSTM_HEADS
    ng = 2 * MLSTM_HEADS
    tile = pl.BlockSpec((None, chunk, dmix), lambda b, c: (b, c, 0))
    return pl.pallas_call(
        functools.partial(_mlstm_scan_body, chunk=chunk, dh=dh),
        grid=(bsz, seq // chunk),
        in_specs=[tile, tile, tile,
                  pl.BlockSpec((None, chunk, ng), lambda b, c: (b, c, 0)),
                  pl.BlockSpec((None, ng, chunk), lambda b, c: (b, 0, c))],
        out_specs=tile,
        out_shape=jax.ShapeDtypeStruct((bsz, seq, dmix), BF16),
        scratch_shapes=[pltpu.VMEM((MLSTM_HEADS, dh, dh), F32),
                        pltpu.VMEM((MLSTM_HEADS, 1, dh), F32),
                        pltpu.VMEM((MLSTM_HEADS, 8, LANES), F32)],
        compiler_params=_params("parallel", "arbitrary"),
        name="mlstm_scan",
    )(q, k, v, gcol, grow)


def _silu(z):
    return z * _sigmoid(z)


def _attention_head_into(y_ref, rows, h, qm_ref, kv_ref, z_ref, zcol0, col0):
    dx = XATTN_HEADS * XATTN_HEAD_DIM
    lo, hi = h * XATTN_HEAD_DIM, (h + 1) * XATTN_HEAD_DIM
    s = _dot_nt(qm_ref[rows, lo:hi], kv_ref[:, lo:hi]) * (XATTN_HEAD_DIM ** -0.5)
    p = jnp.exp(s - jnp.max(s, axis=-1, keepdims=True))
    o = _dot(_bf(p), kv_ref[:, dx + lo:dx + hi]) / jnp.sum(p, axis=-1, keepdims=True)
    y_ref[:, col0 + lo:col0 + hi] = _bf(o * _silu(_f32(z_ref[rows, zcol0 + lo:zcol0 + hi])))


def _gate_then_project(pieces_of, project, nsub, nblk):
    for piece in pieces_of(0):
        piece()
    for s in range(nsub):
        upcoming = pieces_of(s + 1) if s + 1 < nsub else []
        per_block = -(-len(upcoming) // nblk)
        for n in range(nblk):
            project(s, n)
            for piece in upcoming[n * per_block:(n + 1) * per_block]:
                piece()


def _mlstm_out_body(hs_ref, xc_ref, qm_ref, z_ref, x_ref, kv_ref, wo_ref, mg_ref, sk_ref,
                    o_ref, ya_ref, yb_ref, *, dh, sub):
    dmix = MLSTM_HEADS * dh
    rows_of = lambda s: slice(s * sub, (s + 1) * sub)
    y_of = lambda s: (ya_ref, yb_ref)[s % 2]

    def norm_head(y_ref, rows, h):
        lo, hi = h * dh, (h + 1) * dh
        seg = _f32(hs_ref[rows, lo:hi])
        d = seg - jnp.mean(seg, axis=-1, keepdims=True)
        var = jnp.mean(d * d, axis=-1, keepdims=True)
        ymix = d * lax.rsqrt(var + MHLN_EPS) * mg_ref[:, lo:hi] + sk_ref[:, lo:hi] * _f32(xc_ref[rows, lo:hi])
        y_ref[:, lo:hi] = _bf(ymix * _silu(_f32(z_ref[rows, lo:hi])))

    def pieces_of(s):
        rows = rows_of(s)
        return ([functools.partial(norm_head, y_of(s), rows, h) for h in range(MLSTM_HEADS)]
                + [functools.partial(_attention_head_into, y_of(s), rows, h, qm_ref, kv_ref, z_ref, dmix, dmix)
                   for h in range(XATTN_HEADS)])

    def project(s, n):
        rows, cols = rows_of(s), slice(n * MXU_DIM, (n + 1) * MXU_DIM)
        o_ref[rows, cols] = x_ref[rows, cols] + _dot(y_of(s)[...], wo_ref[:, cols])

    _gate_then_project(pieces_of, project, o_ref.shape[0] // sub, o_ref.shape[1] // MXU_DIM)


def _mlstm_out(hs, xc, proj, x, kv, w_out, mhn_g, skip, *, tm):
    bsz, seq, dmix = hs.shape
    dmodel = x.shape[-1]
    dx = XATTN_HEADS * XATTN_HEAD_DIM
    dinner = dmix + dx
    nmem = kv.shape[1]
    full = lambda shape: pl.BlockSpec(shape, lambda b, s: (0,) * len(shape))
    tile = lambda w, j: pl.BlockSpec((None, tm, w), lambda b, s: (b, s, j))
    return pl.pallas_call(
        functools.partial(_mlstm_out_body, dh=dmix // MLSTM_HEADS, sub=OUT_SUB_ROWS),
        grid=(bsz, seq // tm),
        in_specs=[tile(dmix, 0), tile(dmix, 0),
                  tile(dx, dmix // dx), tile(dinner, 1), tile(dmodel, 0),
                  pl.BlockSpec((None, nmem, 2 * dx), lambda b, s: (b, 0, 0)),
                  full(w_out.shape), full((1, dmix)), full((1, dmix))],
        out_specs=tile(dmodel, 0),
        out_shape=jax.ShapeDtypeStruct((bsz, seq, dmodel), F32),
        scratch_shapes=[pltpu.VMEM((OUT_SUB_ROWS, dinner), BF16)] * 2,
        compiler_params=_params("parallel", "parallel"),
        name="mlstm_out",
    )(hs, xc, proj, proj, x, kv, w_out, mhn_g.reshape(1, dmix), skip.reshape(1, dmix))


def _head_dot(x, e):
    return _dot(_bf(x), e)


def _head_sums(x, e_ref):
    tm = x.shape[0]
    nslab = x.shape[1] // LANES
    rows = jnp.concatenate([x[:, i * LANES:(i + 1) * LANES] for i in range(nslab)], axis=0)
    sums = _head_dot(rows, e_ref[...])
    return jnp.concatenate([sums[i * tm:(i + 1) * tm] for i in range(nslab)], axis=1)


def _token_shift(x, carry_ref, mu):
    tm = x.shape[0]
    prev = pltpu.roll(x, 1, axis=0)
    first = lax.broadcasted_iota(jnp.int32, (tm, 1), 0) == 0
    prev = jnp.where(first, carry_ref[7:8, :], prev)
    carry_ref[...] = x[tm - 8:tm, :]
    return x + (prev - x) * mu


def _rwkv_pre_body(pr_ref, pk_ref, pv_ref, pl_ref, vf_ref, mur_ref, muk_ref, muv_ref, mul_ref,
                   wl2_ref, vecs_ref, e_ref,
                   r_ref, lw_ref, k_ref, v_ref, kk_ref, ab_ref, g_ref, bonus_ref,
                   cr_ref, ck_ref, cv_ref, cl_ref):
    @pl.when(pl.program_id(1) == 0)
    def _():
        for c in (cr_ref, ck_ref, cv_ref, cl_ref):
            c[...] = jnp.zeros(c.shape, F32)

    r = _token_shift(_f32(pr_ref[...]), cr_ref, mur_ref[...])
    k = _token_shift(_f32(pk_ref[...]), ck_ref, muk_ref[...])
    v = _token_shift(_f32(pv_ref[...]), cv_ref, muv_ref[...])
    lo = _token_shift(_f32(pl_ref[...]), cl_ref, mul_ref[...])
    w0, a0, v0, k_k, k_a, r_k = (vecs_ref[i:i + 1, :] for i in range(6))

    seg = lambda i: lo[:, i * LORA_SEG:(i + 1) * LORA_SEG]
    d = w0 + _dot(_bf(jnp.tanh(seg(0))), wl2_ref[0])
    lw_ref[...] = -math.exp(-0.5) * _sigmoid(d)
    a = _sigmoid(a0 + _dot(_bf(seg(1)), wl2_ref[1]))
    v = v + (_f32(vf_ref[...]) - v) * _sigmoid(v0 + _dot(_bf(seg(2)), wl2_ref[2]))
    g_ref[...] = _bf(_dot(_bf(_sigmoid(seg(3))), wl2_ref[3]))

    kk = k * k_k
    kk = kk * lax.rsqrt(jnp.maximum(_head_sums(kk * kk, e_ref), L2_EPS * L2_EPS))
    k2 = k * (1.0 + (a - 1.0) * k_a)
    r_ref[...] = _bf(r)
    k_ref[...] = _bf(k2)
    v_ref[...] = _bf(v)
    kk_ref[...] = _bf(kk)
    ab_ref[...] = _bf(kk * a)
    bonus_ref[...] = _bf(_head_sums(r * k2 * r_k, e_ref) * v)


def _rwkv_pre(proj, v_first, mus, wl2, vecs, e2, *, tm, lora_block):
    bsz, seq, dmix = v_first.shape
    lw = 4 * LORA_SEG
    full = lambda shape: pl.BlockSpec(shape, lambda b, s: (0,) * len(shape))
    tile = lambda w, j: pl.BlockSpec((None, tm, w), lambda b, s: (b, s, j))
    act = jax.ShapeDtypeStruct((bsz, seq, dmix), BF16)
    log_decay = jax.ShapeDtypeStruct((bsz, seq, dmix), F32)
    return pl.pallas_call(
        _rwkv_pre_body,
        grid=(bsz, seq // tm),
        in_specs=[tile(dmix, 0), tile(dmix, 1), tile(dmix, 2), tile(lw, lora_block), tile(dmix, 0),
                  full((1, dmix)), full((1, dmix)), full((1, dmix)), full((1, lw)),
                  full(wl2.shape), full(vecs.shape), full(e2.shape)],
        out_specs=[tile(dmix, 0)] * 8,
        out_shape=[act, log_decay] + [act] * 6,
        scratch_shapes=[pltpu.VMEM((8, dmix), F32)] * 3 + [pltpu.VMEM((8, lw), F32)],
        compiler_params=_params("parallel", "arbitrary"),
        name="rwkv_pre",
    )(proj, proj, proj, proj, v_first, *mus, wl2, vecs, e2)


def _wkv7_body(r_ref, lw_ref, k_ref, v_ref, kk_ref, ab_ref, o_ref, h_ref, *, chunk, npair, nsub):
    @pl.when(pl.program_id(1) == 0)
    def _():
        h_ref[...] = jnp.zeros(h_ref.shape, F32)

    c2 = 2 * chunk
    row = lax.broadcasted_iota(jnp.int32, (chunk, chunk), 0)
    col = lax.broadcasted_iota(jnp.int32, (chunk, chunk), 1)
    tril = (col <= row).astype(F32)
    def scaled_operands(rows, cols):
        lw = lw_ref[rows, cols]
        g_inc = sum(_dot(_bf(tril), t) for t in _bf16_terms(lw))
        g_last = g_inc[chunk - 1:chunk, :]
        e_neg = jnp.exp(-g_inc)
        e_end = jnp.exp(g_last - g_inc)
        kk = _f32(kk_ref[rows, cols])
        ab = _f32(ab_ref[rows, cols])
        kx = _f32(k_ref[rows, cols])
        return (-kk * jnp.exp(g_inc - lw),
                ab * e_neg,
                kx * e_neg,
                _f32(r_ref[rows, cols]) * jnp.exp(g_inc),
                ab * e_end,
                kx * e_end,
                jnp.exp(g_last))

    lane_head = lax.broadcasted_iota(jnp.int32, (c2, LANES), 1) // RWKV_HEAD_DIM
    row_head = lax.broadcasted_iota(jnp.int32, (c2, LANES), 0) // chunk
    head_mask = lane_head == row_head
    ri = lax.broadcasted_iota(jnp.int32, (c2, c2), 0)
    ci = lax.broadcasted_iota(jnp.int32, (c2, c2), 1)
    same = (ri // chunk) == (ci // chunk)
    tp = lax.broadcasted_iota(jnp.int32, (chunk, c2), 0)
    sp = lax.broadcasted_iota(jnp.int32, (chunk, c2), 1) % chunk
    strict = sp < tp
    incl = sp <= tp
    eye_p = (sp == tp).astype(F32)
    li = lax.broadcasted_iota(jnp.int32, (LANES, LANES), 0)
    lj = lax.broadcasted_iota(jnp.int32, (LANES, LANES), 1)
    lane_eye = (li == lj).astype(F32)
    lane_same = (li // RWKV_HEAD_DIM) == (lj // RWKV_HEAD_DIM)

    def stack(x):
        return _bf(jnp.where(head_mask, jnp.concatenate([x, x], axis=0), 0.0))

    def blocks(x):
        return _bf(jnp.where(same, jnp.concatenate([x, x], axis=0), 0.0))

    def cat0(*xs):
        return jnp.concatenate(xs, axis=0)

    def cat1(*xs):
        return jnp.concatenate(xs, axis=1)

    each = lambda f, *xs: [f(*t) for t in zip(*xs)]
    nlevel = int(math.log2(chunk))
    sls = [slice(p * LANES, (p + 1) * LANES) for p in range(npair)]
    rows_of = lambda s: slice(s * chunk, (s + 1) * chunk)
    operands = [scaled_operands(rows_of(0), sl) for sl in sls]
    for sub in range(nsub):
        rows = rows_of(sub)
        a_t, b_t, k_t, r_t, b_e, k_e, gam_l = map(list, zip(*operands))
        operands = []
        todo = list(sls) if sub + 1 < nsub else []

        def feed(n=1):
            for _ in range(min(n, len(todo))):
                operands.append(scaled_operands(rows_of(sub + 1), todo.pop(0)))

        v_s = [stack(_f32(v_ref[rows, sl])) for sl in sls]
        sc = each(lambda a, r, b, k: _dot_nt(_bf(cat0(a, r)), cat0(stack(b), stack(k))), a_t, r_t, b_t, k_t)
        feed()
        d_ab = [jnp.where(strict, s[:chunk, :c2], 0.0) for s in sc]
        d_ak = [_bf(jnp.where(strict, s[:chunk, c2:], 0.0)) for s in sc]
        d_rb = [_bf(jnp.where(incl, s[chunk:, :c2], 0.0)) for s in sc]
        d_rk = [_bf(jnp.where(incl, s[chunk:, c2:], 0.0)) for s in sc]
        t_inv = [eye_p + d for d in d_ab]
        d_pow = d_ab
        for level in range(nlevel):
            rhs = each(blocks, d_pow)
            if level == 0:
                d_pow = each(lambda d, w: _dot(_bf(d), w), d_pow, rhs)
            elif level < nlevel - 1:
                both = each(lambda d, t, w: _dot(_bf(cat0(d, t)), w), d_pow, t_inv, rhs)
                d_pow = [x[:chunk] for x in both]
                t_inv = each(lambda t, x: t + x[chunk:], t_inv, both)
            else:
                t_inv = each(lambda t, w: t + _dot(_bf(t), w), t_inv, rhs)
            feed()
        akv = each(_dot, d_ak, v_s)
        feed()
        wu = each(lambda t, a, x: _dot(_bf(t), cat1(stack(a), stack(x))), t_inv, a_t, akv)
        feed()
        ry = each(lambda drb, x: _dot(drb, cat1(stack(x[:, :LANES]), stack(x[:, LANES:]))), d_rb, wu)
        feed()
        y0 = each(lambda x, drk, v: x[:, LANES:] + _dot(drk, v), ry, d_rk, v_s)
        rp = each(lambda r, x: r + x[:, :LANES], r_t, ry)
        feed()
        pq = each(lambda be, x: _dot_tn(_bf(be), _bf(x)), b_e, wu)
        feed()
        kv = [_dot_tn(_bf(ke), v_ref[rows, sl]) for ke, sl in zip(k_e, sls)]
        feed(len(todo))
        p_mat = each(lambda g, x: lane_eye * g + jnp.where(lane_same, x[:, :LANES], 0.0), gam_l, pq)
        q_mat = each(lambda x, y: jnp.where(lane_same, x[:, LANES:] + y, 0.0), pq, kv)
        for i, sl in enumerate(sls):
            yh = _dot(_bf(cat0(rp[i], p_mat[i])), _bf(h_ref[i]))
            o_ref[rows, sl] = _bf(yh[:chunk] + y0[i])
            h_ref[i] = yh[chunk:] + q_mat[i]


def _wkv7(r, lw, k, v, kk, ab, *, chunk):
    bsz, seq, dmix = r.shape
    npair = dmix // LANES
    nsub = WKV_CHUNKS_PER_STEP
    tile = pl.BlockSpec((None, nsub * chunk, dmix), lambda b, c: (b, c, 0))
    return pl.pallas_call(
        functools.partial(_wkv7_body, chunk=chunk, npair=npair, nsub=nsub),
        grid=(bsz, seq // (nsub * chunk)),
        in_specs=[tile] * 6,
        out_specs=tile,
        out_shape=jax.ShapeDtypeStruct((bsz, seq, dmix), BF16),
        scratch_shapes=[pltpu.VMEM((npair, LANES, LANES), F32)],
        compiler_params=_params("parallel", "arbitrary"),
        name="wkv7",
    )(r, lw, k, v, kk, ab)


def _rwkv_out_body(o_ref, g_ref, bonus_ref, qm_ref, za_ref, zb_ref, x_ref, kv_ref, wo_ref,
                   lg_ref, lb_ref, fg_ref, e_ref, out_ref, ya_ref, yb_ref, *, dmix, sub):
    half = za_ref.shape[1]
    inv_n = 1.0 / RWKV_HEAD_DIM
    nblk = out_ref.shape[1] // MXU_DIM
    rows_of = lambda s: slice(s * sub, (s + 1) * sub)
    y_of = lambda s: (ya_ref, yb_ref)[s % 2]

    def norm_tile(y_ref, rows, i):
        sl = slice(i * LANES, (i + 1) * LANES)
        seg = _f32(o_ref[rows, sl])
        d = seg - _head_dot(seg, e_ref[...]) * inv_n
        var = _head_dot(d * d, e_ref[...]) * inv_n
        ymix = ((d * lax.rsqrt(var + RWKV_GN_EPS) * lg_ref[:, sl] + lb_ref[:, sl] + _f32(bonus_ref[rows, sl]))
                * _f32(g_ref[rows, sl]))
        z = (za_ref[rows, sl] if (i + 1) * LANES <= half
             else zb_ref[rows, i * LANES - half:(i + 1) * LANES - half])
        y_ref[:, sl] = _bf(ymix * _silu(_f32(z)))

    def pieces_of(s):
        rows = rows_of(s)
        return ([functools.partial(norm_tile, y_of(s), rows, i) for i in range(dmix // LANES)]
                + [functools.partial(_attention_head_into, y_of(s), rows, h, qm_ref, kv_ref, zb_ref,
                                     dmix - half, dmix) for h in range(XATTN_HEADS)])

    def project(s, n):
        rows, cols = rows_of(s), slice(n * MXU_DIM, (n + 1) * MXU_DIM)
        out_ref[rows, cols] = x_ref[rows, cols] + _dot(y_of(s)[...], wo_ref[:, cols])
        if n == nblk - 1:
            xn = out_ref[rows, :]
            ms = jnp.mean(xn * xn, axis=-1, keepdims=True)
            out_ref[rows, :] = xn * lax.rsqrt(ms + RMS_EPS) * fg_ref[...]

    _gate_then_project(pieces_of, project, out_ref.shape[0] // sub, nblk)


def _rwkv_out(o, g, bonus, proj, x, kv, w_out, lnx_g, lnx_b, final_g, e2, *, tm, qm_block, z_block):
    bsz, seq, dmix = o.shape
    dmodel = x.shape[-1]
    dx = XATTN_HEADS * XATTN_HEAD_DIM
    dinner = dmix + dx
    half = dinner // 2
    nmem = kv.shape[1]
    full = lambda shape: pl.BlockSpec(shape, lambda b, s: (0,) * len(shape))
    tile = lambda w, j: pl.BlockSpec((None, tm, w), lambda b, s: (b, s, j))
    return pl.pallas_call(
        functools.partial(_rwkv_out_body, dmix=dmix, sub=OUT_SUB_ROWS),
        grid=(bsz, seq // tm),
        in_specs=[tile(dmix, 0), tile(dmix, 0), tile(dmix, 0),
                  tile(dx, qm_block), tile(half, z_block), tile(half, z_block + 1), tile(dmodel, 0),
                  pl.BlockSpec((None, nmem, 2 * dx), lambda b, s: (b, 0, 0)),
                  full(w_out.shape), full((1, dmix)), full((1, dmix)), full((1, dmodel)), full(e2.shape)],
        out_specs=tile(dmodel, 0),
        out_shape=jax.ShapeDtypeStruct((bsz, seq, dmodel), F32),
        scratch_shapes=[pltpu.VMEM((OUT_SUB_ROWS, dinner), BF16)] * 2,
        compiler_params=_params("parallel", "parallel"),
        name="rwkv_out",
    )(o, g, bonus, proj, proj, proj, x, kv, w_out, lnx_g.reshape(1, dmix), lnx_b.reshape(1, dmix),
      final_g.reshape(1, dmodel), e2)


def _blockdiag_tiles(w, tile):
    nb, blk, _ = w.shape
    rows = w.reshape(nb * blk // tile, tile, blk)
    idx = jnp.arange(tile)
    on_diag = (idx[:, None] // blk) == (idx[None, :] // blk)
    return jnp.where(on_diag, jnp.tile(rows, (1, 1, tile // blk)), 0.0)


def _pad_rows(w, rows):
    return jnp.pad(w, ((0, rows - w.shape[0]), (0, 0)))


def _pad_cols(w, cols):
    return jnp.pad(w, ((0, 0), (0, cols - w.shape[1])))


def kernel(x, mem, norm_g, mem_norm_g, mem_kv_w, w_out, ml_w_in, ml_conv_w, ml_conv_b, ml_wq, ml_wk, ml_wv, ml_w_gate, ml_b_gate, ml_mhn_g, ml_skip, rw_w_in, rw_mu, rw_w_lora2, rw_w0, rw_a_lora2, rw_a0, rw_v_lora2, rw_v0, rw_g_lora2, rw_k_k, rw_k_a, rw_r_k, rw_lnx_g, rw_lnx_b, final_g):
    bsz, seq, dmodel = x.shape
    nmem = mem.shape[1]
    dmix = ml_conv_w.shape[-1]
    dx = XATTN_HEADS * XATTN_HEAD_DIM
    ntok = bsz * seq
    mem2d = mem.reshape(bsz * nmem, dmodel)

    kv0 = _rms_matmul(mem2d, mem_norm_g[0], _bf(mem_kv_w[0]), tm=512, tn=2 * dx,
                      out_dtype=BF16).reshape(bsz, nmem, 2 * dx)
    proj0 = _rms_matmul(x.reshape(ntok, dmodel), norm_g[0], _bf(ml_w_in[0]), tm=1024, tn=2048,
                        out_dtype=BF16).reshape(bsz, seq, -1)
    wqk = jnp.concatenate([_blockdiag_tiles(ml_wq[0], MXU_DIM), _blockdiag_tiles(ml_wk[0], MXU_DIM)], axis=-1)
    wv = _blockdiag_tiles(ml_wv[0], MXU_DIM)
    wg = ml_w_gate[0].reshape(3, dmix, -1)
    xc, q, k, v0, gcol, grow = _mlstm_pre(
        proj0, ml_conv_w[0], ml_conv_b[0], _bf(wqk), _bf(wv),
        _bf(jnp.pad(wg, ((0, 0), (0, 0), (0, LANES - wg.shape[-1])))),
        _bf(jnp.pad(wg.transpose(0, 2, 1), ((0, 0), (0, 16 - wg.shape[-1]), (0, 0)))),
        ml_b_gate[0], ts=256)
    hs = _mlstm_scan(q, k, v0, gcol, grow, chunk=MLSTM_CHUNK)
    x1 = _mlstm_out(hs, xc, proj0, x, kv0, _bf(w_out[0]), ml_mhn_g[0], ml_skip[0], tm=2 * OUT_SUB_ROWS)

    w_in = rw_w_in[0]
    mu = rw_mu[0]
    cuts = [0, 3 * dmix]
    for rank in (DECAY_RANK, ICLR_RANK, VRES_RANK, GATE_RANK):
        cuts.append(cuts[-1] + rank)
    lora_w = jnp.concatenate([_pad_cols(w_in[:, cuts[i]:cuts[i + 1]], LORA_SEG) for i in range(1, 5)], axis=1)
    lora_mu = jnp.concatenate([jnp.pad(mu[cuts[i]:cuts[i + 1]], (0, LORA_SEG - (cuts[i + 1] - cuts[i])))
                               for i in range(1, 5)])
    w_packed = jnp.concatenate([w_in[:, :3 * dmix], w_in[:, cuts[-1]:], lora_w], axis=1)
    qm_block = (3 * dmix) // dx
    z_block = (3 * dmix + dx) // ((dmix + dx) // 2)
    lora_block = (3 * dmix + dx + dmix + dx) // (4 * LORA_SEG)

    kv1 = _rms_matmul(mem2d, mem_norm_g[1], _bf(mem_kv_w[1]), tm=512, tn=2 * dx,
                      out_dtype=BF16).reshape(bsz, nmem, 2 * dx)
    proj1 = _rms_matmul(x1.reshape(ntok, dmodel), norm_g[1], _bf(w_packed), tm=1024, tn=2560,
                        out_dtype=BF16).reshape(bsz, seq, -1)
    wl2 = _bf(jnp.stack([_pad_rows(rw_w_lora2[0], LORA_SEG), _pad_rows(rw_a_lora2[0], LORA_SEG),
                         _pad_rows(rw_v_lora2[0], LORA_SEG), _pad_rows(rw_g_lora2[0], LORA_SEG)]))
    vecs = jnp.stack([rw_w0[0], rw_a0[0], rw_v0[0], rw_k_k[0], rw_k_a[0], rw_r_k[0].reshape(-1),
                      jnp.zeros_like(rw_w0[0]), jnp.zeros_like(rw_w0[0])])
    lane = jnp.arange(LANES) // RWKV_HEAD_DIM
    e2 = _bf(lane[:, None] == lane[None, :])
    mus = (mu[:dmix].reshape(1, -1), mu[dmix:2 * dmix].reshape(1, -1), mu[2 * dmix:3 * dmix].reshape(1, -1),
           lora_mu.reshape(1, -1))
    r, lw, k2, v1, kk, ab, g, bonus = _rwkv_pre(proj1, v0, mus, wl2, vecs, e2, tm=256, lora_block=lora_block)
    o = _wkv7(r, lw, k2, v1, kk, ab, chunk=WKV_CHUNK)
    return _rwkv_out(o, g, bonus, proj1, x1, kv1, _bf(w_out[1]), rw_lnx_g[0], rw_lnx_b[0], final_g, e2,
                     tm=2 * OUT_SUB_ROWS, qm_block=qm_block, z_block=z_block)
```

```python
import functools
import math

import jax
import jax.numpy as jnp
from jax import lax
from jax.experimental import pallas as pl
from jax.experimental.pallas import tpu as pltpu

F32 = jnp.float32
BF16 = jnp.bfloat16

LANES = 128
MXU_DIM = 256
VMEM_LIMIT = 48 * 1024 * 1024

XATTN_HEADS = 4
XATTN_HEAD_DIM = 128
MLSTM_HEADS = 4
MLSTM_CONV = 4
QKV_BLOCK = 4
RWKV_HEAD_DIM = 64
DECAY_RANK, ICLR_RANK, VRES_RANK, GATE_RANK = 64, 64, 32, 128
LORA_SEG = 128

RMS_EPS = 1e-6
MHLN_EPS = 1e-5
RWKV_GN_EPS = 64e-5
L2_EPS = 1e-12

MLSTM_CHUNK = 256
WKV_CHUNK = 64
WKV_CHUNKS_PER_STEP = 4
OUT_SUB_ROWS = 256


def _params(*sem):
    return pltpu.CompilerParams(dimension_semantics=sem, vmem_limit_bytes=VMEM_LIMIT)


def _sigmoid(x):
    return 0.5 * jnp.tanh(0.5 * x) + 0.5


def _log_sigmoid(x):
    return jnp.minimum(x, 0.0) - jnp.log1p(jnp.exp(-jnp.abs(x)))


def _dot(a, b):
    return jnp.dot(a, b, preferred_element_type=F32)


def _dot_nt(a, b):
    return lax.dot_general(a, b, (((1,), (1,)), ((), ())), preferred_element_type=F32)


def _dot_tn(a, b):
    return lax.dot_general(a, b, (((0,), (0,)), ((), ())), preferred_element_type=F32)


def _bf(x):
    return x.astype(BF16)


def _f32(x):
    return x.astype(F32)


def _bf16_terms(x):
    hi = _bf(x)
    rest = x - _f32(hi)
    mid = _bf(rest)
    return hi, mid, _bf(rest - _f32(mid))


def _rms_matmul_body(*refs, seq_rows):
    if seq_rows is None:
        x_ref, g_ref, w_ref, o_ref, h_ref = refs
    else:
        x_ref, g_ref, w_ref, mu_ref, o_ref, h_ref, carry_ref = refs
    i, j = pl.program_id(0), pl.program_id(1)

    @pl.when(j == 0)
    def _():
        x = x_ref[...]
        ms = jnp.mean(x * x, axis=-1, keepdims=True)
        h_ref[...] = _bf(x * lax.rsqrt(ms + RMS_EPS) * g_ref[...])

    if seq_rows is None:
        o_ref[...] = _dot(h_ref[...], w_ref[...]).astype(o_ref.dtype)
        return

    tm, tn = o_ref.shape

    @pl.when(lax.rem(i * tm, seq_rows) == 0)
    def _():
        carry_ref[j] = jnp.zeros(carry_ref.shape[1:], F32)

    first_row = lax.broadcasted_iota(jnp.int32, (tm, 1), 0) == 0
    for nb in range(tn // MXU_DIM):
        cols = slice(nb * MXU_DIM, (nb + 1) * MXU_DIM)
        acc = _dot(h_ref[...], w_ref[:, cols])
        prev = jnp.where(first_row, carry_ref[j, 7:8, cols], pltpu.roll(acc, 1, axis=0))
        carry_ref[j, :, cols] = acc[tm - 8:tm, :]
        o_ref[:, cols] = (acc + (prev - acc) * mu_ref[:, cols]).astype(o_ref.dtype)


def _rms_matmul(x2d, g, w, *, tm, tn, out_dtype, shift_mu=None, seq_rows=None):
    m, d = x2d.shape
    n = w.shape[1]
    in_specs = [pl.BlockSpec((tm, d), lambda i, j: (i, 0)),
                pl.BlockSpec((1, d), lambda i, j: (0, 0)),
                pl.BlockSpec((d, tn), lambda i, j: (0, j))]
    args = [x2d, g.reshape(1, d), w]
    scratch = [pltpu.VMEM((tm, d), BF16)]
    semantics = ("parallel", "arbitrary")
    if shift_mu is not None:
        assert seq_rows % tm == 0
        in_specs.append(pl.BlockSpec((1, tn), lambda i, j: (0, j)))
        args.append(shift_mu.reshape(1, n))
        scratch.append(pltpu.VMEM((n // tn, 8, tn), F32))
        semantics = ("arbitrary", "arbitrary")
    return pl.pallas_call(
        functools.partial(_rms_matmul_body, seq_rows=seq_rows if shift_mu is not None else None),
        grid=(m // tm, n // tn),
        in_specs=in_specs,
        out_specs=pl.BlockSpec((tm, tn), lambda i, j: (i, j)),
        out_shape=jax.ShapeDtypeStruct((m, n), out_dtype),
        scratch_shapes=scratch,
        compiler_params=_params(*semantics),
        name="rms_matmul",
    )(*args)


def _mlstm_pre_body(u_ref, cw_ref, cb_ref, wqk_ref, wv_ref, wg_ref, wgt_ref, bgr_ref, bgc_ref,
                    xc_ref, q_ref, k_ref, v_ref, gcol_ref, grow_ref, ext_ref, *, ts, nblk):
    s = pl.program_id(1)

    @pl.when(s == 0)
    def _():
        ext_ref[0:8, :] = jnp.zeros((8, ext_ref.shape[1]), F32)

    @pl.when(s > 0)
    def _():
        ext_ref[0:8, :] = ext_ref[ts:ts + 8, :]

    ext_ref[8:8 + ts, :] = _f32(u_ref[...])
    gc = bgr_ref[...]
    gr = jnp.zeros((wgt_ref.shape[1], ts), F32)
    for b in range(nblk):
        lo, hi = b * MXU_DIM, (b + 1) * MXU_DIM
        ub = u_ref[:, lo:hi]
        acc = cb_ref[:, lo:hi] + cw_ref[MLSTM_CONV - 1:MLSTM_CONV, lo:hi] * _f32(ub)
        for j in range(1, MLSTM_CONV):
            acc = acc + cw_ref[MLSTM_CONV - 1 - j:MLSTM_CONV - j, lo:hi] * ext_ref[8 - j:8 - j + ts, lo:hi]
        xcb = _bf(acc * _sigmoid(acc))
        xc_ref[:, lo:hi] = xcb
        qk = _dot(xcb, wqk_ref[b])
        qb, kb, vb = _bf(qk[:, :MXU_DIM]), _bf(qk[:, MXU_DIM:]), _bf(_dot(ub, wv_ref[b]))
        q_ref[:, lo:hi] = qb
        k_ref[:, lo:hi] = kb
        v_ref[:, lo:hi] = vb
        gc = gc + _dot(qb, wg_ref[0, lo:hi, :]) + _dot(kb, wg_ref[1, lo:hi, :]) + _dot(vb, wg_ref[2, lo:hi, :])
        gr = (gr + _dot_nt(wgt_ref[0, :, lo:hi], qb) + _dot_nt(wgt_ref[1, :, lo:hi], kb)
              + _dot_nt(wgt_ref[2, :, lo:hi], vb))
    lane = lax.broadcasted_iota(jnp.int32, gc.shape, 1)
    gc = jnp.where(lane < MLSTM_HEADS, gc, _log_sigmoid(gc))
    gcol_ref[...] = gc[:, :2 * MLSTM_HEADS]

    gr = gr[:2 * MLSTM_HEADS, :] + bgc_ref[...]
    row = lax.broadcasted_iota(jnp.int32, gr.shape, 0)
    grow_ref[...] = jnp.where(row < MLSTM_HEADS, gr, _log_sigmoid(gr))


def _mlstm_pre(proj, conv_w, conv_b, wqk, wv, wg, wgt, bg, *, ts):
    bsz, seq, _ = proj.shape
    dmix = conv_w.shape[1]
    nblk = dmix // MXU_DIM
    ng = 2 * MLSTM_HEADS
    full = lambda shape: pl.BlockSpec(shape, lambda b, s: (0,) * len(shape))
    tile = pl.BlockSpec((None, ts, dmix), lambda b, s: (b, s, 0))
    act = jax.ShapeDtypeStruct((bsz, seq, dmix), BF16)
    return pl.pallas_call(
        functools.partial(_mlstm_pre_body, ts=ts, nblk=nblk),
        grid=(bsz, seq // ts),
        in_specs=[tile, full(conv_w.shape), full((1, dmix)), full(wqk.shape), full(wv.shape),
                  full(wg.shape), full(wgt.shape), full((1, LANES)), full((ng, 1))],
        out_specs=[tile, tile, tile, tile,
                   pl.BlockSpec((None, ts, ng), lambda b, s: (b, s, 0)),
                   pl.BlockSpec((None, ng, ts), lambda b, s: (b, 0, s))],
        out_shape=[act, act, act, act,
                   jax.ShapeDtypeStruct((bsz, seq, ng), F32),
                   jax.ShapeDtypeStruct((bsz, ng, seq), F32)],
        scratch_shapes=[pltpu.VMEM((ts + 8, dmix), F32)],
        compiler_params=_params("parallel", "arbitrary"),
        name="mlstm_pre",
    )(proj, conv_w, conv_b.reshape(1, dmix), wqk, wv, wg, wgt,
      jnp.pad(bg, (0, LANES - ng)).reshape(1, LANES), bg.reshape(ng, 1))


def _mlstm_scan_body(q_ref, k_ref, v_ref, gcol_ref, grow_ref, h_ref, c_ref, n_ref, m_ref, *, chunk, dh):
    @pl.when(pl.program_id(1) == 0)
    def _():
        c_ref[...] = jnp.zeros(c_ref.shape, F32)
        n_ref[...] = jnp.zeros(n_ref.shape, F32)
        m_ref[...] = jnp.zeros(m_ref.shape, F32)

    gc = gcol_ref[...]
    gr = grow_ref[...]
    row = lax.broadcasted_iota(jnp.int32, (chunk, chunk), 0)
    col = lax.broadcasted_iota(jnp.int32, (chunk, chunk), 1)
    causal = col <= row
    tril = causal.astype(F32)
    bcol_all = sum(_dot(_bf(tril), t) for t in _bf16_terms(gc))
    brow_all = sum(_dot_nt(t, _bf(tril)) for t in _bf16_terms(gr))
    scale = dh ** -0.5
    cols = lambda h: slice(h * dh, (h + 1) * dh)
    st = [dict() for _ in range(MLSTM_HEADS)]

    def products(h):
        t = st[h]
        t['kh'] = _f32(k_ref[:, cols(h)]) * scale
        t['qk'] = _dot_nt(q_ref[:, cols(h)], _bf(t['kh']))
        t['qc'] = _dot(q_ref[:, cols(h)], _bf(c_ref[h]))

    def gates(h):
        t = st[h]
        i_col = gc[:, h:h + 1]
        b_col = bcol_all[:, MLSTM_HEADS + h:MLSTM_HEADS + h + 1]
        i_row = gr[h:h + 1, :]
        b_row = brow_all[MLSTM_HEADS + h:MLSTM_HEADS + h + 1, :]
        m_old = m_ref[h][0:1, 0:1]
        d_mat = jnp.where(causal, b_col - b_row + i_row, -jnp.inf)
        inter = b_col + m_old
        m_t = jnp.maximum(jnp.max(d_mat, axis=-1, keepdims=True), inter)
        t['e'] = jnp.exp(d_mat - m_t)
        t['g_in'] = jnp.exp(inter - m_t)
        t['floor'] = jnp.exp(-m_t)
        b_last = b_col[chunk - 1:chunk, :]
        w_s = b_last - b_col + i_col
        t['m_new'] = jnp.maximum(b_last + m_old, jnp.max(w_s, axis=0, keepdims=True))
        t['k_scale'] = jnp.exp(w_s - t['m_new'])
        t['g_old'] = jnp.exp(b_last + m_old - t['m_new'])

    def outputs(h):
        t = st[h]
        s = t['qk'] * t['e']
        num = _dot(_bf(s), v_ref[:, cols(h)]) + t['g_in'] * t['qc']
        den = (jnp.sum(s, axis=-1, keepdims=True)
               + t['g_in'] * jnp.sum(_f32(q_ref[:, cols(h)]) * n_ref[h], axis=-1, keepdims=True))
        h_ref[:, cols(h)] = _bf(num / jnp.maximum(jnp.abs(den), t['floor']))

    def state_update(h):
        t = st[h]
        kw = t['kh'] * t['k_scale']
        c_ref[h] = t['g_old'] * c_ref[h] + _dot_tn(_bf(kw), v_ref[:, cols(h)])
        n_ref[h] = t['g_old'] * n_ref[h] + jnp.sum(kw, axis=0, keepdims=True)
        m_ref[h] = jnp.broadcast_to(t['m_new'], m_ref.shape[1:])

    stages = (products, gates, outputs, state_update)
    for step in range(MLSTM_HEADS + len(stages) - 1):
        for depth, stage in enumerate(stages):
            if 0 <= step - depth < MLSTM_HEADS:
                stage(step - depth)


def _mlstm_scan(q, k, v, gcol, grow, *, chunk):
    bsz, seq, dmix = q.shape
    dh = dmix // MLSTM_HEADS
    ng = 2 * MLSTM_HEADS
    tile = pl.BlockSpec((None, chunk, dmix), lambda b, c: (b, c, 0))
    return pl.pallas_call(
        functools.partial(_mlstm_scan_body, chunk=chunk, dh=dh),
        grid=(bsz, seq // chunk),
        in_specs=[tile, tile, tile,
                  pl.BlockSpec((None, chunk, ng), lambda b, c: (b, c, 0)),
                  pl.BlockSpec((None, ng, chunk), lambda b, c: (b, 0, c))],
        out_specs=tile,
        out_shape=jax.ShapeDtypeStruct((bsz, seq, dmix), BF16),
        scratch_shapes=[pltpu.VMEM((MLSTM_HEADS, dh, dh), F32),
                        pltpu.VMEM((MLSTM_HEADS, 1, dh), F32),
                        pltpu.VMEM((MLSTM_HEADS, 8, LANES), F32)],
        compiler_params=_params("parallel", "arbitrary"),
        name="mlstm_scan",
    )(q, k, v, gcol, grow)


def _silu(z):
    return z * _sigmoid(z)


def _attention_head_into(y_ref, rows, h, qm_ref, kv_ref, z_ref, zcol0, col0):
    dx = XATTN_HEADS * XATTN_HEAD_DIM
    lo, hi = h * XATTN_HEAD_DIM, (h + 1) * XATTN_HEAD_DIM
    s = _dot_nt(qm_ref[rows, lo:hi], kv_ref[:, lo:hi]) * (XATTN_HEAD_DIM ** -0.5)
    p = jnp.exp(s - jnp.max(s, axis=-1, keepdims=True))
    o = _dot(_bf(p), kv_ref[:, dx + lo:dx + hi]) / jnp.sum(p, axis=-1, keepdims=True)
    y_ref[:, col0 + lo:col0 + hi] = _bf(o * _silu(_f32(z_ref[rows, zcol0 + lo:zcol0 + hi])))


def _gate_then_project(pieces_of, project, nsub, nblk):
    for piece in pieces_of(0):
        piece()
    for s in range(nsub):
        upcoming = pieces_of(s + 1) if s + 1 < nsub else []
        per_block = -(-len(upcoming) // nblk)
        for n in range(nblk):
            project(s, n)
            for piece in upcoming[n * per_block:(n + 1) * per_block]:
                piece()


def _mlstm_out_body(hs_ref, xc_ref, qm_ref, z_ref, x_ref, kv_ref, wo_ref, mg_ref, sk_ref,
                    o_ref, ya_ref, yb_ref, *, dh, sub):
    dmix = MLSTM_HEADS * dh
    rows_of = lambda s: slice(s * sub, (s + 1) * sub)
    y_of = lambda s: (ya_ref, yb_ref)[s % 2]

    def norm_head(y_ref, rows, h):
        lo, hi = h * dh, (h + 1) * dh
        seg = _f32(hs_ref[rows, lo:hi])
        d = seg - jnp.mean(seg, axis=-1, keepdims=True)
        var = jnp.mean(d * d, axis=-1, keepdims=True)
        ymix = d * lax.rsqrt(var + MHLN_EPS) * mg_ref[:, lo:hi] + sk_ref[:, lo:hi] * _f32(xc_ref[rows, lo:hi])
        y_ref[:, lo:hi] = _bf(ymix * _silu(_f32(z_ref[rows, lo:hi])))

    def pieces_of(s):
        rows = rows_of(s)
        return ([functools.partial(norm_head, y_of(s), rows, h) for h in range(MLSTM_HEADS)]
                + [functools.partial(_attention_head_into, y_of(s), rows, h, qm_ref, kv_ref, z_ref, dmix, dmix)
                   for h in range(XATTN_HEADS)])

    def project(s, n):
        rows, cols = rows_of(s), slice(n * MXU_DIM, (n + 1) * MXU_DIM)
        o_ref[rows, cols] = x_ref[rows, cols] + _dot(y_of(s)[...], wo_ref[:, cols])

    _gate_then_project(pieces_of, project, o_ref.shape[0] // sub, o_ref.shape[1] // MXU_DIM)


def _mlstm_out(hs, xc, proj, x, kv, w_out, mhn_g, skip, *, tm):
    bsz, seq, dmix = hs.shape
    dmodel = x.shape[-1]
    dx = XATTN_HEADS * XATTN_HEAD_DIM
    dinner = dmix + dx
    nmem = kv.shape[1]
    full = lambda shape: pl.BlockSpec(shape, lambda b, s: (0,) * len(shape))
    tile = lambda w, j: pl.BlockSpec((None, tm, w), lambda b, s: (b, s, j))
    return pl.pallas_call(
        functools.partial(_mlstm_out_body, dh=dmix // MLSTM_HEADS, sub=OUT_SUB_ROWS),
        grid=(bsz, seq // tm),
        in_specs=[tile(dmix, 0), tile(dmix, 0),
                  tile(dx, dmix // dx), tile(dinner, 1), tile(dmodel, 0),
                  pl.BlockSpec((None, nmem, 2 * dx), lambda b, s: (b, 0, 0)),
                  full(w_out.shape), full((1, dmix)), full((1, dmix))],
        out_specs=tile(dmodel, 0),
        out_shape=jax.ShapeDtypeStruct((bsz, seq, dmodel), F32),
        scratch_shapes=[pltpu.VMEM((OUT_SUB_ROWS, dinner), BF16)] * 2,
        compiler_params=_params("parallel", "parallel"),
        name="mlstm_out",
    )(hs, xc, proj, proj, x, kv, w_out, mhn_g.reshape(1, dmix), skip.reshape(1, dmix))


def _head_dot(x, e):
    return _dot(_bf(x), e)


def _head_sums(x, e_ref):
    tm = x.shape[0]
    nslab = x.shape[1] // LANES
    rows = jnp.concatenate([x[:, i * LANES:(i + 1) * LANES] for i in range(nslab)], axis=0)
    sums = _head_dot(rows, e_ref[...])
    return jnp.concatenate([sums[i * tm:(i + 1) * tm] for i in range(nslab)], axis=1)


def _rwkv_pre_body(pr_ref, pk_ref, pv_ref, pl_ref, vf_ref, wl2_ref, vecs_ref, e_ref,
                   r_ref, lw_ref, k_ref, v_ref, kk_ref, ab_ref, g_ref, bonus_ref):
    r, k, v, lo = _f32(pr_ref[...]), _f32(pk_ref[...]), _f32(pv_ref[...]), _f32(pl_ref[...])
    w0, a0, v0, k_k, k_a, r_k = (vecs_ref[i:i + 1, :] for i in range(6))

    seg = lambda i: lo[:, i * LORA_SEG:(i + 1) * LORA_SEG]
    d = w0 + _dot(_bf(jnp.tanh(seg(0))), wl2_ref[0])
    lw_ref[...] = -math.exp(-0.5) * _sigmoid(d)
    a = _sigmoid(a0 + _dot(_bf(seg(1)), wl2_ref[1]))
    v = v + (_f32(vf_ref[...]) - v) * _sigmoid(v0 + _dot(_bf(seg(2)), wl2_ref[2]))
    g_ref[...] = _bf(_dot(_bf(_sigmoid(seg(3))), wl2_ref[3]))

    kk = k * k_k
    kk = kk * lax.rsqrt(jnp.maximum(_head_sums(kk * kk, e_ref), L2_EPS * L2_EPS))
    k2 = k * (1.0 + (a - 1.0) * k_a)
    r_ref[...] = _bf(r)
    k_ref[...] = _bf(k2)
    v_ref[...] = _bf(v)
    kk_ref[...] = _bf(kk)
    ab_ref[...] = _bf(kk * a)
    bonus_ref[...] = _bf(_head_sums(r * k2 * r_k, e_ref) * v)


def _rwkv_pre(proj, v_first, wl2, vecs, e2, *, tm, lora_block):
    bsz, seq, dmix = v_first.shape
    lw = 4 * LORA_SEG
    full = lambda shape: pl.BlockSpec(shape, lambda b, s: (0,) * len(shape))
    tile = lambda w, j: pl.BlockSpec((None, tm, w), lambda b, s: (b, s, j))
    act = jax.ShapeDtypeStruct((bsz, seq, dmix), BF16)
    log_decay = jax.ShapeDtypeStruct((bsz, seq, dmix), F32)
    return pl.pallas_call(
        _rwkv_pre_body,
        grid=(bsz, seq // tm),
        in_specs=[tile(dmix, 0), tile(dmix, 1), tile(dmix, 2), tile(lw, lora_block), tile(dmix, 0),
                  full(wl2.shape), full(vecs.shape), full(e2.shape)],
        out_specs=[tile(dmix, 0)] * 8,
        out_shape=[act, log_decay] + [act] * 6,
        compiler_params=_params("parallel", "parallel"),
        name="rwkv_pre",
    )(proj, proj, proj, proj, v_first, wl2, vecs, e2)


def _wkv7_body(r_ref, lw_ref, k_ref, v_ref, kk_ref, ab_ref, o_ref, h_ref, *, chunk, npair, nsub):
    @pl.when(pl.program_id(1) == 0)
    def _():
        h_ref[...] = jnp.zeros(h_ref.shape, F32)

    c2 = 2 * chunk
    row = lax.broadcasted_iota(jnp.int32, (chunk, chunk), 0)
    col = lax.broadcasted_iota(jnp.int32, (chunk, chunk), 1)
    tril = (col <= row).astype(F32)
    def scaled_operands(rows, cols):
        lw = lw_ref[rows, cols]
        g_inc = sum(_dot(_bf(tril), t) for t in _bf16_terms(lw))
        g_last = g_inc[chunk - 1:chunk, :]
        e_neg = jnp.exp(-g_inc)
        e_end = jnp.exp(g_last - g_inc)
        kk = _f32(kk_ref[rows, cols])
        ab = _f32(ab_ref[rows, cols])
        kx = _f32(k_ref[rows, cols])
        return (-kk * jnp.exp(g_inc - lw),
                ab * e_neg,
                kx * e_neg,
                _f32(r_ref[rows, cols]) * jnp.exp(g_inc),
                ab * e_end,
                kx * e_end,
                jnp.exp(g_last))

    lane_head = lax.broadcasted_iota(jnp.int32, (c2, LANES), 1) // RWKV_HEAD_DIM
    row_head = lax.broadcasted_iota(jnp.int32, (c2, LANES), 0) // chunk
    head_mask = lane_head == row_head
    ri = lax.broadcasted_iota(jnp.int32, (c2, c2), 0)
    ci = lax.broadcasted_iota(jnp.int32, (c2, c2), 1)
    same = (ri // chunk) == (ci // chunk)
    tp = lax.broadcasted_iota(jnp.int32, (chunk, c2), 0)
    sp = lax.broadcasted_iota(jnp.int32, (chunk, c2), 1) % chunk
    strict = sp < tp
    incl = sp <= tp
    eye_p = (sp == tp).astype(F32)
    li = lax.broadcasted_iota(jnp.int32, (LANES, LANES), 0)
    lj = lax.broadcasted_iota(jnp.int32, (LANES, LANES), 1)
    lane_eye = (li == lj).astype(F32)
    lane_same = (li // RWKV_HEAD_DIM) == (lj // RWKV_HEAD_DIM)

    def stack(x):
        return _bf(jnp.where(head_mask, jnp.concatenate([x, x], axis=0), 0.0))

    def blocks(x):
        return _bf(jnp.where(same, jnp.concatenate([x, x], axis=0), 0.0))

    def cat0(*xs):
        return jnp.concatenate(xs, axis=0)

    def cat1(*xs):
        return jnp.concatenate(xs, axis=1)

    each = lambda f, *xs: [f(*t) for t in zip(*xs)]
    nlevel = int(math.log2(chunk))
    sls = [slice(p * LANES, (p + 1) * LANES) for p in range(npair)]
    rows_of = lambda s: slice(s * chunk, (s + 1) * chunk)
    operands = [scaled_operands(rows_of(0), sl) for sl in sls]
    for sub in range(nsub):
        rows = rows_of(sub)
        a_t, b_t, k_t, r_t, b_e, k_e, gam_l = map(list, zip(*operands))
        operands = []
        todo = list(sls) if sub + 1 < nsub else []

        def feed(n=1):
            for _ in range(min(n, len(todo))):
                operands.append(scaled_operands(rows_of(sub + 1), todo.pop(0)))

        v_s = [stack(_f32(v_ref[rows, sl])) for sl in sls]
        sc = each(lambda a, r, b, k: _dot_nt(_bf(cat0(a, r)), cat0(stack(b), stack(k))), a_t, r_t, b_t, k_t)
        feed()
        d_ab = [jnp.where(strict, s[:chunk, :c2], 0.0) for s in sc]
        d_ak = [_bf(jnp.where(strict, s[:chunk, c2:], 0.0)) for s in sc]
        d_rb = [_bf(jnp.where(incl, s[chunk:, :c2], 0.0)) for s in sc]
        d_rk = [_bf(jnp.where(incl, s[chunk:, c2:], 0.0)) for s in sc]
        t_inv = [eye_p + d for d in d_ab]
        d_pow = d_ab
        for level in range(nlevel):
            rhs = each(blocks, d_pow)
            if level == 0:
                d_pow = each(lambda d, w: _dot(_bf(d), w), d_pow, rhs)
            elif level < nlevel - 1:
                both = each(lambda d, t, w: _dot(_bf(cat0(d, t)), w), d_pow, t_inv, rhs)
                d_pow = [x[:chunk] for x in both]
                t_inv = each(lambda t, x: t + x[chunk:], t_inv, both)
            else:
                t_inv = each(lambda t, w: t + _dot(_bf(t), w), t_inv, rhs)
            feed()
        akv = each(_dot, d_ak, v_s)
        feed()
        wu = each(lambda t, a, x: _dot(_bf(t), cat1(stack(a), stack(x))), t_inv, a_t, akv)
        feed()
        ry = each(lambda drb, x: _dot(drb, cat1(stack(x[:, :LANES]), stack(x[:, LANES:]))), d_rb, wu)
        feed()
        y0 = each(lambda x, drk, v: x[:, LANES:] + _dot(drk, v), ry, d_rk, v_s)
        rp = each(lambda r, x: r + x[:, :LANES], r_t, ry)
        feed()
        pq = each(lambda be, x: _dot_tn(_bf(be), _bf(x)), b_e, wu)
        feed()
        kv = [_dot_tn(_bf(ke), v_ref[rows, sl]) for ke, sl in zip(k_e, sls)]
        feed(len(todo))
        p_mat = each(lambda g, x: lane_eye * g + jnp.where(lane_same, x[:, :LANES], 0.0), gam_l, pq)
        q_mat = each(lambda x, y: jnp.where(lane_same, x[:, LANES:] + y, 0.0), pq, kv)
        for i, sl in enumerate(sls):
            yh = _dot(_bf(cat0(rp[i], p_mat[i])), _bf(h_ref[i]))
            o_ref[rows, sl] = _bf(yh[:chunk] + y0[i])
            h_ref[i] = yh[chunk:] + q_mat[i]


def _wkv7(r, lw, k, v, kk, ab, *, chunk):
    bsz, seq, dmix = r.shape
    npair = dmix // LANES
    nsub = WKV_CHUNKS_PER_STEP
    tile = pl.BlockSpec((None, nsub * chunk, dmix), lambda b, c: (b, c, 0))
    return pl.pallas_call(
        functools.partial(_wkv7_body, chunk=chunk, npair=npair, nsub=nsub),
        grid=(bsz, seq // (nsub * chunk)),
        in_specs=[tile] * 6,
        out_specs=tile,
        out_shape=jax.ShapeDtypeStruct((bsz, seq, dmix), BF16),
        scratch_shapes=[pltpu.VMEM((npair, LANES, LANES), F32)],
        compiler_params=_params("parallel", "arbitrary"),
        name="wkv7",
    )(r, lw, k, v, kk, ab)


def _rwkv_out_body(o_ref, g_ref, bonus_ref, qm_ref, za_ref, zb_ref, x_ref, kv_ref, wo_ref,
                   lg_ref, lb_ref, fg_ref, e_ref, out_ref, ya_ref, yb_ref, *, dmix, sub):
    half = za_ref.shape[1]
    inv_n = 1.0 / RWKV_HEAD_DIM
    nblk = out_ref.shape[1] // MXU_DIM
    rows_of = lambda s: slice(s * sub, (s + 1) * sub)
    y_of = lambda s: (ya_ref, yb_ref)[s % 2]

    def norm_tile(y_ref, rows, i):
        sl = slice(i * LANES, (i + 1) * LANES)
        seg = _f32(o_ref[rows, sl])
        d = seg - _head_dot(seg, e_ref[...]) * inv_n
        var = _head_dot(d * d, e_ref[...]) * inv_n
        ymix = ((d * lax.rsqrt(var + RWKV_GN_EPS) * lg_ref[:, sl] + lb_ref[:, sl] + _f32(bonus_ref[rows, sl]))
                * _f32(g_ref[rows, sl]))
        z = (za_ref[rows, sl] if (i + 1) * LANES <= half
             else zb_ref[rows, i * LANES - half:(i + 1) * LANES - half])
        y_ref[:, sl] = _bf(ymix * _silu(_f32(z)))

    def pieces_of(s):
        rows = rows_of(s)
        return ([functools.partial(norm_tile, y_of(s), rows, i) for i in range(dmix // LANES)]
                + [functools.partial(_attention_head_into, y_of(s), rows, h, qm_ref, kv_ref, zb_ref,
                                     dmix - half, dmix) for h in range(XATTN_HEADS)])

    def project(s, n):
        rows, cols = rows_of(s), slice(n * MXU_DIM, (n + 1) * MXU_DIM)
        out_ref[rows, cols] = x_ref[rows, cols] + _dot(y_of(s)[...], wo_ref[:, cols])
        if n == nblk - 1:
            xn = out_ref[rows, :]
            ms = jnp.mean(xn * xn, axis=-1, keepdims=True)
            out_ref[rows, :] = xn * lax.rsqrt(ms + RMS_EPS) * fg_ref[...]

    _gate_then_project(pieces_of, project, out_ref.shape[0] // sub, nblk)


def _rwkv_out(o, g, bonus, proj, x, kv, w_out, lnx_g, lnx_b, final_g, e2, *, tm, qm_block, z_block):
    bsz, seq, dmix = o.shape
    dmodel = x.shape[-1]
    dx = XATTN_HEADS * XATTN_HEAD_DIM
    dinner = dmix + dx
    half = dinner // 2
    nmem = kv.shape[1]
    full = lambda shape: pl.BlockSpec(shape, lambda b, s: (0,) * len(shape))
    tile = lambda w, j: pl.BlockSpec((None, tm, w), lambda b, s: (b, s, j))
    return pl.pallas_call(
        functools.partial(_rwkv_out_body, dmix=dmix, sub=OUT_SUB_ROWS),
        grid=(bsz, seq // tm),
        in_specs=[tile(dmix, 0), tile(dmix, 0), tile(dmix, 0),
                  tile(dx, qm_block), tile(half, z_block), tile(half, z_block + 1), tile(dmodel, 0),
                  pl.BlockSpec((None, nmem, 2 * dx), lambda b, s: (b, 0, 0)),
                  full(w_out.shape), full((1, dmix)), full((1, dmix)), full((1, dmodel)), full(e2.shape)],
        out_specs=tile(dmodel, 0),
        out_shape=jax.ShapeDtypeStruct((bsz, seq, dmodel), F32),
        scratch_shapes=[pltpu.VMEM((OUT_SUB_ROWS, dinner), BF16)] * 2,
        compiler_params=_params("parallel", "parallel"),
        name="rwkv_out",
    )(o, g, bonus, proj, proj, proj, x, kv, w_out, lnx_g.reshape(1, dmix), lnx_b.reshape(1, dmix),
      final_g.reshape(1, dmodel), e2)


def _blockdiag_tiles(w, tile):
    nb, blk, _ = w.shape
    rows = w.reshape(nb * blk // tile, tile, blk)
    idx = jnp.arange(tile)
    on_diag = (idx[:, None] // blk) == (idx[None, :] // blk)
    return jnp.where(on_diag, jnp.tile(rows, (1, 1, tile // blk)), 0.0)


def _pad_rows(w, rows):
    return jnp.pad(w, ((0, rows - w.shape[0]), (0, 0)))


def _pad_cols(w, cols):
    return jnp.pad(w, ((0, 0), (0, cols - w.shape[1])))


def kernel(x, mem, norm_g, mem_norm_g, mem_kv_w, w_out, ml_w_in, ml_conv_w, ml_conv_b, ml_wq, ml_wk, ml_wv, ml_w_gate, ml_b_gate, ml_mhn_g, ml_skip, rw_w_in, rw_mu, rw_w_lora2, rw_w0, rw_a_lora2, rw_a0, rw_v_lora2, rw_v0, rw_g_lora2, rw_k_k, rw_k_a, rw_r_k, rw_lnx_g, rw_lnx_b, final_g):
    bsz, seq, dmodel = x.shape
    nmem = mem.shape[1]
    dmix = ml_conv_w.shape[-1]
    dx = XATTN_HEADS * XATTN_HEAD_DIM
    ntok = bsz * seq
    mem2d = mem.reshape(bsz * nmem, dmodel)

    kv0 = _rms_matmul(mem2d, mem_norm_g[0], _bf(mem_kv_w[0]), tm=512, tn=2 * dx,
                      out_dtype=BF16).reshape(bsz, nmem, 2 * dx)
    proj0 = _rms_matmul(x.reshape(ntok, dmodel), norm_g[0], _bf(ml_w_in[0]), tm=1024, tn=2048,
                        out_dtype=BF16).reshape(bsz, seq, -1)
    wqk = jnp.concatenate([_blockdiag_tiles(ml_wq[0], MXU_DIM), _blockdiag_tiles(ml_wk[0], MXU_DIM)], axis=-1)
    wv = _blockdiag_tiles(ml_wv[0], MXU_DIM)
    wg = ml_w_gate[0].reshape(3, dmix, -1)
    xc, q, k, v0, gcol, grow = _mlstm_pre(
        proj0, ml_conv_w[0], ml_conv_b[0], _bf(wqk), _bf(wv),
        _bf(jnp.pad(wg, ((0, 0), (0, 0), (0, LANES - wg.shape[-1])))),
        _bf(jnp.pad(wg.transpose(0, 2, 1), ((0, 0), (0, 16 - wg.shape[-1]), (0, 0)))),
        ml_b_gate[0], ts=256)
    hs = _mlstm_scan(q, k, v0, gcol, grow, chunk=MLSTM_CHUNK)
    x1 = _mlstm_out(hs, xc, proj0, x, kv0, _bf(w_out[0]), ml_mhn_g[0], ml_skip[0], tm=2 * OUT_SUB_ROWS)

    w_in = rw_w_in[0]
    mu = rw_mu[0]
    cuts = [0, 3 * dmix]
    for rank in (DECAY_RANK, ICLR_RANK, VRES_RANK, GATE_RANK):
        cuts.append(cuts[-1] + rank)
    lora_w = jnp.concatenate([_pad_cols(w_in[:, cuts[i]:cuts[i + 1]], LORA_SEG) for i in range(1, 5)], axis=1)
    lora_mu = jnp.concatenate([jnp.pad(mu[cuts[i]:cuts[i + 1]], (0, LORA_SEG - (cuts[i + 1] - cuts[i])))
                               for i in range(1, 5)])
    w_packed = jnp.concatenate([w_in[:, :3 * dmix], w_in[:, cuts[-1]:], lora_w], axis=1)
    qm_block = (3 * dmix) // dx
    z_block = (3 * dmix + dx) // ((dmix + dx) // 2)
    lora_block = (3 * dmix + dx + dmix + dx) // (4 * LORA_SEG)

    kv1 = _rms_matmul(mem2d, mem_norm_g[1], _bf(mem_kv_w[1]), tm=512, tn=2 * dx,
                      out_dtype=BF16).reshape(bsz, nmem, 2 * dx)
    mu_packed = jnp.concatenate([mu[:3 * dmix], jnp.zeros((w_in.shape[1] - cuts[-1],), F32), lora_mu])
    proj1 = _rms_matmul(x1.reshape(ntok, dmodel), norm_g[1], _bf(w_packed), tm=1024, tn=2560,
                        out_dtype=BF16, shift_mu=mu_packed, seq_rows=seq).reshape(bsz, seq, -1)
    wl2 = _bf(jnp.stack([_pad_rows(rw_w_lora2[0], LORA_SEG), _pad_rows(rw_a_lora2[0], LORA_SEG),
                         _pad_rows(rw_v_lora2[0], LORA_SEG), _pad_rows(rw_g_lora2[0], LORA_SEG)]))
    vecs = jnp.stack([rw_w0[0], rw_a0[0], rw_v0[0], rw_k_k[0], rw_k_a[0], rw_r_k[0].reshape(-1),
                      jnp.zeros_like(rw_w0[0]), jnp.zeros_like(rw_w0[0])])
    lane = jnp.arange(LANES) // RWKV_HEAD_DIM
    e2 = _bf(lane[:, None] == lane[None, :])
    r, lw, k2, v1, kk, ab, g, bonus = _rwkv_pre(proj1, v0, wl2, vecs, e2, tm=256, lora_block=lora_block)
    o = _wkv7(r, lw, k2, v1, kk, ab, chunk=WKV_CHUNK)
    return _rwkv_out(o, g, bonus, proj1, x1, kv1, _bf(w_out[1]), rw_lnx_g[0], rw_lnx_b[0], final_g, e2,
                     tm=2 * OUT_SUB_ROWS, qm_block=qm_block, z_block=z_block)
```

```python
import functools
import math

import jax
import jax.numpy as jnp
from jax import lax
from jax.experimental import pallas as pl
from jax.experimental.pallas import tpu as pltpu

F32 = jnp.float32
BF16 = jnp.bfloat16

LANES = 128
MXU_DIM = 256
VMEM_LIMIT = 48 * 1024 * 1024

XATTN_HEADS = 4
XATTN_HEAD_DIM = 128
MLSTM_HEADS = 4
MLSTM_CONV = 4
QKV_BLOCK = 4
RWKV_HEAD_DIM = 64
DECAY_RANK, ICLR_RANK, VRES_RANK, GATE_RANK = 64, 64, 32, 128
LORA_SEG = 128

RMS_EPS = 1e-6
MHLN_EPS = 1e-5
RWKV_GN_EPS = 64e-5
L2_EPS = 1e-12

MLSTM_CHUNK = 256
WKV_CHUNK = 64
WKV_CHUNKS_PER_STEP = 8
WKV_CHUNKS_TOGETHER = 2
OUT_TILE_ROWS = 512
OUT_SUB_ROWS = 256


def _params(*sem):
    return pltpu.CompilerParams(dimension_semantics=sem, vmem_limit_bytes=VMEM_LIMIT)


def _sigmoid(x):
    return 0.5 * jnp.tanh(0.5 * x) + 0.5


def _log_sigmoid(x):
    return jnp.minimum(x, 0.0) - jnp.log1p(jnp.exp(-jnp.abs(x)))


def _dot(a, b):
    return jnp.dot(a, b, preferred_element_type=F32)


def _dot_nt(a, b):
    return lax.dot_general(a, b, (((1,), (1,)), ((), ())), preferred_element_type=F32)


def _dot_tn(a, b):
    return lax.dot_general(a, b, (((0,), (0,)), ((), ())), preferred_element_type=F32)


def _bf(x):
    return x.astype(BF16)


def _f32(x):
    return x.astype(F32)


def _bf16_terms(x):
    hi = _bf(x)
    rest = x - _f32(hi)
    mid = _bf(rest)
    return hi, mid, _bf(rest - _f32(mid))


def _rms_matmul_body(x_ref, g_ref, w_ref, o_ref, h_ref):
    @pl.when(pl.program_id(1) == 0)
    def _():
        x = x_ref[...]
        ms = jnp.mean(x * x, axis=-1, keepdims=True)
        h_ref[...] = _bf(x * lax.rsqrt(ms + RMS_EPS) * g_ref[...])

    o_ref[...] = _dot(h_ref[...], w_ref[...]).astype(o_ref.dtype)


def _rms_matmul(x2d, g, w, *, tm, tn, out_dtype):
    m, d = x2d.shape
    n = w.shape[1]
    return pl.pallas_call(
        _rms_matmul_body,
        grid=(m // tm, n // tn),
        in_specs=[pl.BlockSpec((tm, d), lambda i, j: (i, 0)),
                  pl.BlockSpec((1, d), lambda i, j: (0, 0)),
                  pl.BlockSpec((d, tn), lambda i, j: (0, j))],
        out_specs=pl.BlockSpec((tm, tn), lambda i, j: (i, j)),
        out_shape=jax.ShapeDtypeStruct((m, n), out_dtype),
        scratch_shapes=[pltpu.VMEM((tm, d), BF16)],
        compiler_params=_params("parallel", "arbitrary"),
        name="rms_matmul",
    )(x2d, g.reshape(1, d), w)


def _mlstm_pre_body(u_ref, cw_ref, cb_ref, wqk_ref, wv_ref, wg_ref, wgt_ref, bgr_ref, bgc_ref,
                    xc_ref, q_ref, k_ref, v_ref, gcol_ref, grow_ref, ext_ref, *, ts, nblk):
    s = pl.program_id(1)

    @pl.when(s == 0)
    def _():
        ext_ref[0:8, :] = jnp.zeros((8, ext_ref.shape[1]), F32)

    @pl.when(s > 0)
    def _():
        ext_ref[0:8, :] = ext_ref[ts:ts + 8, :]

    ext_ref[8:8 + ts, :] = _f32(u_ref[...])
    gc = bgr_ref[...]
    gr = jnp.zeros((wgt_ref.shape[1], ts), F32)
    for b in range(nblk):
        lo, hi = b * MXU_DIM, (b + 1) * MXU_DIM
        ub = u_ref[:, lo:hi]
        acc = cb_ref[:, lo:hi] + cw_ref[MLSTM_CONV - 1:MLSTM_CONV, lo:hi] * _f32(ub)
        for j in range(1, MLSTM_CONV):
            acc = acc + cw_ref[MLSTM_CONV - 1 - j:MLSTM_CONV - j, lo:hi] * ext_ref[8 - j:8 - j + ts, lo:hi]
        xcb = _bf(acc * _sigmoid(acc))
        xc_ref[:, lo:hi] = xcb
        qk = _dot(xcb, wqk_ref[b])
        qb, kb, vb = _bf(qk[:, :MXU_DIM]), _bf(qk[:, MXU_DIM:]), _bf(_dot(ub, wv_ref[b]))
        q_ref[:, lo:hi] = qb
        k_ref[:, lo:hi] = kb
        v_ref[:, lo:hi] = vb
        gc = gc + _dot(qb, wg_ref[0, lo:hi, :]) + _dot(kb, wg_ref[1, lo:hi, :]) + _dot(vb, wg_ref[2, lo:hi, :])
        gr = (gr + _dot_nt(wgt_ref[0, :, lo:hi], qb) + _dot_nt(wgt_ref[1, :, lo:hi], kb)
              + _dot_nt(wgt_ref[2, :, lo:hi], vb))
    lane = lax.broadcasted_iota(jnp.int32, gc.shape, 1)
    gc = jnp.where(lane < MLSTM_HEADS, gc, _log_sigmoid(gc))
    gcol_ref[...] = gc[:, :2 * MLSTM_HEADS]

    gr = gr[:2 * MLSTM_HEADS, :] + bgc_ref[...]
    row = lax.broadcasted_iota(jnp.int32, gr.shape, 0)
    grow_ref[...] = jnp.where(row < MLSTM_HEADS, gr, _log_sigmoid(gr))


def _mlstm_pre(proj, conv_w, conv_b, wqk, wv, wg, wgt, bg, *, ts):
    bsz, seq, _ = proj.shape
    dmix = conv_w.shape[1]
    nblk = dmix // MXU_DIM
    ng = 2 * MLSTM_HEADS
    full = lambda shape: pl.BlockSpec(shape, lambda b, s: (0,) * len(shape))
    tile = pl.BlockSpec((None, ts, dmix), lambda b, s: (b, s, 0))
    act = jax.ShapeDtypeStruct((bsz, seq, dmix), BF16)
    return pl.pallas_call(
        functools.partial(_mlstm_pre_body, ts=ts, nblk=nblk),
        grid=(bsz, seq // ts),
        in_specs=[tile, full(conv_w.shape), full((1, dmix)), full(wqk.shape), full(wv.shape),
                  full(wg.shape), full(wgt.shape), full((1, LANES)), full((ng, 1))],
        out_specs=[tile, tile, tile, tile,
                   pl.BlockSpec((None, ts, ng), lambda b, s: (b, s, 0)),
                   pl.BlockSpec((None, ng, ts), lambda b, s: (b, 0, s))],
        out_shape=[act, act, act, act,
                   jax.ShapeDtypeStruct((bsz, seq, ng), F32),
                   jax.ShapeDtypeStruct((bsz, ng, seq), F32)],
        scratch_shapes=[pltpu.VMEM((ts + 8, dmix), F32)],
        compiler_params=_params("parallel", "arbitrary"),
        name="mlstm_pre",
    )(proj, conv_w, conv_b.reshape(1, dmix), wqk, wv, wg, wgt,
      jnp.pad(bg, (0, LANES - ng)).reshape(1, LANES), bg.reshape(ng, 1))


def _mlstm_scan_body(q_ref, k_ref, v_ref, gcol_ref, grow_ref, h_ref, c_ref, n_ref, m_ref, *, chunk, dh):
    @pl.when(pl.program_id(1) == 0)
    def _():
        c_ref[...] = jnp.zeros(c_ref.shape, F32)
        n_ref[...] = jnp.zeros(n_ref.shape, F32)
        m_ref[...] = jnp.zeros(m_ref.shape, F32)

    gc = gcol_ref[...]
    gr = grow_ref[...]
    row = lax.broadcasted_iota(jnp.int32, (chunk, chunk), 0)
    col = lax.broadcasted_iota(jnp.int32, (chunk, chunk), 1)
    causal = col <= row
    tril = causal.astype(F32)
    bcol_all = sum(_dot(_bf(tril), t) for t in _bf16_terms(gc))
    brow_all = sum(_dot_nt(t, _bf(tril)) for t in _bf16_terms(gr))
    scale = dh ** -0.5
    cols = lambda h: slice(h * dh, (h + 1) * dh)
    st = [dict() for _ in range(MLSTM_HEADS)]

    def products(h):
        t = st[h]
        t['kh'] = _f32(k_ref[:, cols(h)]) * scale
        t['qk'] = _dot_nt(q_ref[:, cols(h)], _bf(t['kh']))
        t['qc'] = _dot(q_ref[:, cols(h)], _bf(c_ref[h]))

    def gates(h):
        t = st[h]
        i_col = gc[:, h:h + 1]
        b_col = bcol_all[:, MLSTM_HEADS + h:MLSTM_HEADS + h + 1]
        i_row = gr[h:h + 1, :]
        b_row = brow_all[MLSTM_HEADS + h:MLSTM_HEADS + h + 1, :]
        m_old = m_ref[h][0:1, 0:1]
        d_mat = jnp.where(causal, b_col - b_row + i_row, -jnp.inf)
        inter = b_col + m_old
        m_t = jnp.maximum(jnp.max(d_mat, axis=-1, keepdims=True), inter)
        t['e'] = jnp.exp(d_mat - m_t)
        t['g_in'] = jnp.exp(inter - m_t)
        t['floor'] = jnp.exp(-m_t)
        b_last = b_col[chunk - 1:chunk, :]
        w_s = b_last - b_col + i_col
        t['m_new'] = jnp.maximum(b_last + m_old, jnp.max(w_s, axis=0, keepdims=True))
        t['k_scale'] = jnp.exp(w_s - t['m_new'])
        t['g_old'] = jnp.exp(b_last + m_old - t['m_new'])

    def outputs(h):
        t = st[h]
        s = t['qk'] * t['e']
        num = _dot(_bf(s), v_ref[:, cols(h)]) + t['g_in'] * t['qc']
        den = (jnp.sum(s, axis=-1, keepdims=True)
               + t['g_in'] * jnp.sum(_f32(q_ref[:, cols(h)]) * n_ref[h], axis=-1, keepdims=True))
        h_ref[:, cols(h)] = _bf(num / jnp.maximum(jnp.abs(den), t['floor']))

    def state_update(h):
        t = st[h]
        kw = t['kh'] * t['k_scale']
        c_ref[h] = t['g_old'] * c_ref[h] + _dot_tn(_bf(kw), v_ref[:, cols(h)])
        n_ref[h] = t['g_old'] * n_ref[h] + jnp.sum(kw, axis=0, keepdims=True)
        m_ref[h] = jnp.broadcast_to(t['m_new'], m_ref.shape[1:])

    stages = (products, gates, outputs, state_update)
    for step in range(MLSTM_HEADS + len(stages) - 1):
        for depth, stage in enumerate(stages):
            if 0 <= step - depth < MLSTM_HEADS:
                stage(step - depth)


def _mlstm_scan(q, k, v, gcol, grow, *, chunk):
    bsz, seq, dmix = q.shape
    dh = dmix // MLSTM_HEADS
    ng = 2 * MLSTM_HEADS
    tile = pl.BlockSpec((None, chunk, dmix), lambda b, c: (b, c, 0))
    return pl.pallas_call(
        functools.partial(_mlstm_scan_body, chunk=chunk, dh=dh),
        grid=(bsz, seq // chunk),
        in_specs=[tile, tile, tile,
                  pl.BlockSpec((None, chunk, ng), lambda b, c: (b, c, 0)),
                  pl.BlockSpec((None, ng, chunk), lambda b, c: (b, 0, c))],
        out_specs=tile,
        out_shape=jax.ShapeDtypeStruct((bsz, seq, dmix), BF16),
        scratch_shapes=[pltpu.VMEM((MLSTM_HEADS, dh, dh), F32),
                        pltpu.VMEM((MLSTM_HEADS, 1, dh), F32),
                        pltpu.VMEM((MLSTM_HEADS, 8, LANES), F32)],
        compiler_params=_params("parallel", "arbitrary"),
        name="mlstm_scan",
    )(q, k, v, gcol, grow)


def _silu(z):
    return z * _sigmoid(z)


def _attention_head_into(y_ref, rows, h, qm_ref, kv_ref, z_ref, zcol0, col0):
    dx = XATTN_HEADS * XATTN_HEAD_DIM
    lo, hi = h * XATTN_HEAD_DIM, (h + 1) * XATTN_HEAD_DIM
    s = _dot_nt(qm_ref[rows, lo:hi], kv_ref[:, lo:hi]) * (XATTN_HEAD_DIM ** -0.5)
    p = jnp.exp(s - jnp.max(s, axis=-1, keepdims=True))
    o = _dot(_bf(p), kv_ref[:, dx + lo:dx + hi]) / jnp.sum(p, axis=-1, keepdims=True)
    y_ref[:, col0 + lo:col0 + hi] = _bf(o * _silu(_f32(z_ref[rows, zcol0 + lo:zcol0 + hi])))


def _gate_then_project(pieces_of, project, nsub, nblk):
    for piece in pieces_of(0):
        piece()
    for s in range(nsub):
        upcoming = pieces_of(s + 1) if s + 1 < nsub else []
        per_block = -(-len(upcoming) // nblk)
        for n in range(nblk):
            project(s, n)
            for piece in upcoming[n * per_block:(n + 1) * per_block]:
                piece()


def _mlstm_out_body(hs_ref, xc_ref, qm_ref, z_ref, x_ref, kv_ref, wo_ref, mg_ref, sk_ref,
                    o_ref, ya_ref, yb_ref, *, dh, sub):
    dmix = MLSTM_HEADS * dh
    rows_of = lambda s: slice(s * sub, (s + 1) * sub)
    y_of = lambda s: (ya_ref, yb_ref)[s % 2]

    def norm_head(y_ref, rows, h):
        lo, hi = h * dh, (h + 1) * dh
        seg = _f32(hs_ref[rows, lo:hi])
        d = seg - jnp.mean(seg, axis=-1, keepdims=True)
        var = jnp.mean(d * d, axis=-1, keepdims=True)
        ymix = d * lax.rsqrt(var + MHLN_EPS) * mg_ref[:, lo:hi] + sk_ref[:, lo:hi] * _f32(xc_ref[rows, lo:hi])
        y_ref[:, lo:hi] = _bf(ymix * _silu(_f32(z_ref[rows, lo:hi])))

    def pieces_of(s):
        rows = rows_of(s)
        return ([functools.partial(norm_head, y_of(s), rows, h) for h in range(MLSTM_HEADS)]
                + [functools.partial(_attention_head_into, y_of(s), rows, h, qm_ref, kv_ref, z_ref, dmix, dmix)
                   for h in range(XATTN_HEADS)])

    def project(s, n):
        rows, cols = rows_of(s), slice(n * MXU_DIM, (n + 1) * MXU_DIM)
        o_ref[rows, cols] = x_ref[rows, cols] + _dot(y_of(s)[...], wo_ref[:, cols])

    _gate_then_project(pieces_of, project, o_ref.shape[0] // sub, o_ref.shape[1] // MXU_DIM)


def _mlstm_out(hs, xc, proj, x, kv, w_out, mhn_g, skip, *, tm):
    bsz, seq, dmix = hs.shape
    dmodel = x.shape[-1]
    dx = XATTN_HEADS * XATTN_HEAD_DIM
    dinner = dmix + dx
    nmem = kv.shape[1]
    full = lambda shape: pl.BlockSpec(shape, lambda b, s: (0,) * len(shape))
    tile = lambda w, j: pl.BlockSpec((None, tm, w), lambda b, s: (b, s, j))
    return pl.pallas_call(
        functools.partial(_mlstm_out_body, dh=dmix // MLSTM_HEADS, sub=OUT_SUB_ROWS),
        grid=(bsz, seq // tm),
        in_specs=[tile(dmix, 0), tile(dmix, 0),
                  tile(dx, dmix // dx), tile(dinner, 1), tile(dmodel, 0),
                  pl.BlockSpec((None, nmem, 2 * dx), lambda b, s: (b, 0, 0)),
                  full(w_out.shape), full((1, dmix)), full((1, dmix))],
        out_specs=tile(dmodel, 0),
        out_shape=jax.ShapeDtypeStruct((bsz, seq, dmodel), F32),
        scratch_shapes=[pltpu.VMEM((OUT_SUB_ROWS, dinner), BF16)] * 2,
        compiler_params=_params("parallel", "parallel"),
        name="mlstm_out",
    )(hs, xc, proj, proj, x, kv, w_out, mhn_g.reshape(1, dmix), skip.reshape(1, dmix))


def _head_dot(x, e):
    return _dot(_bf(x), e)


def _head_sums(x, e_ref):
    tm = x.shape[0]
    nslab = x.shape[1] // LANES
    rows = jnp.concatenate([x[:, i * LANES:(i + 1) * LANES] for i in range(nslab)], axis=0)
    sums = _head_dot(rows, e_ref[...])
    return jnp.concatenate([sums[i * tm:(i + 1) * tm] for i in range(nslab)], axis=1)


def _token_shift(x, carry_ref, mu):
    tm = x.shape[0]
    prev = pltpu.roll(x, 1, axis=0)
    first = lax.broadcasted_iota(jnp.int32, (tm, 1), 0) == 0
    prev = jnp.where(first, carry_ref[7:8, :], prev)
    carry_ref[...] = x[tm - 8:tm, :]
    return x + (prev - x) * mu


def _rwkv_pre_body(pr_ref, pk_ref, pv_ref, pl_ref, vf_ref, mur_ref, muk_ref, muv_ref, mul_ref,
                   wl2_ref, vecs_ref, e_ref,
                   r_ref, lw_ref, k_ref, v_ref, kk_ref, ab_ref, g_ref, bonus_ref,
                   cr_ref, ck_ref, cv_ref, cl_ref):
    @pl.when(pl.program_id(1) == 0)
    def _():
        for c in (cr_ref, ck_ref, cv_ref, cl_ref):
            c[...] = jnp.zeros(c.shape, F32)

    r = _token_shift(_f32(pr_ref[...]), cr_ref, mur_ref[...])
    k = _token_shift(_f32(pk_ref[...]), ck_ref, muk_ref[...])
    v = _token_shift(_f32(pv_ref[...]), cv_ref, muv_ref[...])
    lo = _token_shift(_f32(pl_ref[...]), cl_ref, mul_ref[...])
    w0, a0, v0, k_k, k_a, r_k = (vecs_ref[i:i + 1, :] for i in range(6))

    seg = lambda i: lo[:, i * LORA_SEG:(i + 1) * LORA_SEG]
    d = w0 + _dot(_bf(jnp.tanh(seg(0))), wl2_ref[0])
    lw_ref[...] = -math.exp(-0.5) * _sigmoid(d)
    a = _sigmoid(a0 + _dot(_bf(seg(1)), wl2_ref[1]))
    v = v + (_f32(vf_ref[...]) - v) * _sigmoid(v0 + _dot(_bf(seg(2)), wl2_ref[2]))
    g_ref[...] = _bf(_dot(_bf(_sigmoid(seg(3))), wl2_ref[3]))

    kk = k * k_k
    kk = kk * lax.rsqrt(jnp.maximum(_head_sums(kk * kk, e_ref), L2_EPS * L2_EPS))
    k2 = k * (1.0 + (a - 1.0) * k_a)
    r_ref[...] = _bf(r)
    k_ref[...] = _bf(k2)
    v_ref[...] = _bf(v)
    kk_ref[...] = _bf(kk)
    ab_ref[...] = _bf(kk * a)
    bonus_ref[...] = _bf(_head_sums(r * k2 * r_k, e_ref) * v)


def _rwkv_pre(proj, v_first, mus, wl2, vecs, e2, *, tm, lora_block):
    bsz, seq, dmix = v_first.shape
    lw = 4 * LORA_SEG
    full = lambda shape: pl.BlockSpec(shape, lambda b, s: (0,) * len(shape))
    tile = lambda w, j: pl.BlockSpec((None, tm, w), lambda b, s: (b, s, j))
    act = jax.ShapeDtypeStruct((bsz, seq, dmix), BF16)
    log_decay = jax.ShapeDtypeStruct((bsz, seq, dmix), F32)
    return pl.pallas_call(
        _rwkv_pre_body,
        grid=(bsz, seq // tm),
        in_specs=[tile(dmix, 0), tile(dmix, 1), tile(dmix, 2), tile(lw, lora_block), tile(dmix, 0),
                  full((1, dmix)), full((1, dmix)), full((1, dmix)), full((1, lw)),
                  full(wl2.shape), full(vecs.shape), full(e2.shape)],
        out_specs=[tile(dmix, 0)] * 8,
        out_shape=[act, log_decay] + [act] * 6,
        scratch_shapes=[pltpu.VMEM((8, dmix), F32)] * 3 + [pltpu.VMEM((8, lw), F32)],
        compiler_params=_params("parallel", "arbitrary"),
        name="rwkv_pre",
    )(proj, proj, proj, proj, v_first, *mus, wl2, vecs, e2)


def _wkv7_body(r_ref, lw_ref, k_ref, v_ref, kk_ref, ab_ref, o_ref, h_ref, *, chunk, npair, nsub, together, gw):
    @pl.when(pl.program_id(1) == 0)
    def _():
        h_ref[...] = jnp.zeros(h_ref.shape, F32)

    nh = gw // RWKV_HEAD_DIM
    c2 = nh * chunk
    row = lax.broadcasted_iota(jnp.int32, (chunk, chunk), 0)
    col = lax.broadcasted_iota(jnp.int32, (chunk, chunk), 1)
    tril = (col <= row).astype(F32)
    def scaled_operands(rows, cols):
        lw = lw_ref[rows, cols]
        g_inc = sum(_dot(_bf(tril), t) for t in _bf16_terms(lw))
        g_last = g_inc[chunk - 1:chunk, :]
        e_neg = jnp.exp(-g_inc)
        e_end = jnp.exp(g_last - g_inc)
        kk = _f32(kk_ref[rows, cols])
        ab = _f32(ab_ref[rows, cols])
        kx = _f32(k_ref[rows, cols])
        return (-kk * jnp.exp(g_inc - lw),
                ab * e_neg,
                kx * e_neg,
                _f32(r_ref[rows, cols]) * jnp.exp(g_inc),
                ab * e_end,
                kx * e_end,
                jnp.exp(g_last))

    lane_head = lax.broadcasted_iota(jnp.int32, (c2, gw), 1) // RWKV_HEAD_DIM
    row_head = lax.broadcasted_iota(jnp.int32, (c2, gw), 0) // chunk
    head_mask = lane_head == row_head
    ri = lax.broadcasted_iota(jnp.int32, (c2, c2), 0)
    ci = lax.broadcasted_iota(jnp.int32, (c2, c2), 1)
    same = (ri // chunk) == (ci // chunk)
    tp = lax.broadcasted_iota(jnp.int32, (chunk, c2), 0)
    sp = lax.broadcasted_iota(jnp.int32, (chunk, c2), 1) % chunk
    strict = sp < tp
    incl = sp <= tp
    eye_p = (sp == tp).astype(F32)
    li = lax.broadcasted_iota(jnp.int32, (LANES, LANES), 0)
    lj = lax.broadcasted_iota(jnp.int32, (LANES, LANES), 1)
    lane_eye = (li == lj).astype(F32)
    lane_same = (li // RWKV_HEAD_DIM) == (lj // RWKV_HEAD_DIM)

    def stack(x):
        return _bf(jnp.where(head_mask, jnp.concatenate([x] * nh, axis=0), 0.0))

    def blocks(x):
        return _bf(jnp.where(same, jnp.concatenate([x] * nh, axis=0), 0.0))

    def cat0(*xs):
        return jnp.concatenate(xs, axis=0)

    def cat1(*xs):
        return jnp.concatenate(xs, axis=1)

    each = lambda f, *xs: [f(*t) for t in zip(*xs)]
    nlevel = int(math.log2(chunk))
    ngroup = npair * LANES // gw
    pairs_per_group = gw // LANES
    sls = [slice(g * gw, (g + 1) * gw) for g in range(ngroup)]
    rows_of = lambda s: slice(s * chunk, (s + 1) * chunk)
    items_of = lambda s0: [(rows_of(s), p) for s in range(s0, min(s0 + together, nsub)) for p in range(ngroup)]
    operands = [scaled_operands(rows, sls[p]) for rows, p in items_of(0)]
    for s0 in range(0, nsub, together):
        items = items_of(s0)
        a_t, b_t, k_t, r_t, b_e, k_e, gam_l = map(list, zip(*operands))
        operands = []
        todo = items_of(s0 + together)

        def feed(n=together):
            for _ in range(min(n, len(todo))):
                rows_next, pair_next = todo.pop(0)
                operands.append(scaled_operands(rows_next, sls[pair_next]))

        v_s = [stack(_f32(v_ref[rows, sls[p]])) for rows, p in items]
        sc = each(lambda a, r, b, k: _dot_nt(_bf(cat0(a, r)), cat0(stack(b), stack(k))), a_t, r_t, b_t, k_t)
        feed()
        d_ab = [jnp.where(strict, s[:chunk, :c2], 0.0) for s in sc]
        d_ak = [_bf(jnp.where(strict, s[:chunk, c2:], 0.0)) for s in sc]
        d_rb = [_bf(jnp.where(incl, s[chunk:, :c2], 0.0)) for s in sc]
        d_rk = [_bf(jnp.where(incl, s[chunk:, c2:], 0.0)) for s in sc]
        t_inv = [eye_p + d for d in d_ab]
        d_pow = d_ab
        for level in range(nlevel):
            rhs = each(blocks, d_pow)
            if level == 0:
                d_pow = each(lambda d, w: _dot(_bf(d), w), d_pow, rhs)
            elif level < nlevel - 1:
                both = each(lambda d, t, w: _dot(_bf(cat0(d, t)), w), d_pow, t_inv, rhs)
                d_pow = [x[:chunk] for x in both]
                t_inv = each(lambda t, x: t + x[chunk:], t_inv, both)
            else:
                t_inv = each(lambda t, w: t + _dot(_bf(t), w), t_inv, rhs)
            feed()
        akv = each(_dot, d_ak, v_s)
        feed()
        wu = each(lambda t, a, x: _dot(_bf(t), cat1(stack(a), stack(x))), t_inv, a_t, akv)
        feed()
        ry = each(lambda drb, x: _dot(drb, cat1(stack(x[:, :gw]), stack(x[:, gw:]))), d_rb, wu)
        feed()
        y0 = each(lambda x, drk, v: x[:, gw:] + _dot(drk, v), ry, d_rk, v_s)
        rp = each(lambda r, x: r + x[:, :gw], r_t, ry)
        feed()
        pair_items = [(i, rows, g * pairs_per_group + j, slice(j * LANES, (j + 1) * LANES))
                      for i, (rows, g) in enumerate(items) for j in range(pairs_per_group)]
        abs_lanes = lambda p: slice(p * LANES, (p + 1) * LANES)
        pq = [_dot_tn(_bf(b_e[i][:, pj]), _bf(cat1(wu[i][:, :gw][:, pj], wu[i][:, gw:][:, pj])))
              for i, rows, p, pj in pair_items]
        feed()
        kv = [_dot_tn(_bf(k_e[i][:, pj]), v_ref[rows, abs_lanes(p)]) for i, rows, p, pj in pair_items]
        feed(len(todo))
        p_mat = [lane_eye * gam_l[i][:, pj] + jnp.where(lane_same, x[:, :LANES], 0.0)
                 for (i, rows, p, pj), x in zip(pair_items, pq)]
        q_mat = each(lambda x, y: jnp.where(lane_same, x[:, LANES:] + y, 0.0), pq, kv)
        for n, (i, rows, p, pj) in enumerate(pair_items):
            yh = _dot(_bf(cat0(rp[i][:, pj], p_mat[n])), _bf(h_ref[p]))
            o_ref[rows, abs_lanes(p)] = _bf(yh[:chunk] + y0[i][:, pj])
            h_ref[p] = yh[chunk:] + q_mat[n]


def _wkv7(r, lw, k, v, kk, ab, *, chunk):
    bsz, seq, dmix = r.shape
    npair = dmix // LANES
    nsub = WKV_CHUNKS_PER_STEP
    tile = pl.BlockSpec((None, nsub * chunk, dmix), lambda b, c: (b, c, 0))
    return pl.pallas_call(
        functools.partial(_wkv7_body, chunk=chunk, npair=npair, nsub=nsub, together=WKV_CHUNKS_TOGETHER,
                          gw=LANES),
        grid=(bsz, seq // (nsub * chunk)),
        in_specs=[tile] * 6,
        out_specs=tile,
        out_shape=jax.ShapeDtypeStruct((bsz, seq, dmix), BF16),
        scratch_shapes=[pltpu.VMEM((npair, LANES, LANES), F32)],
        compiler_params=_params("parallel", "arbitrary"),
        name="wkv7",
    )(r, lw, k, v, kk, ab)


def _rwkv_out_body(o_ref, g_ref, bonus_ref, qm_ref, za_ref, zb_ref, x_ref, kv_ref, wo_ref,
                   lg_ref, lb_ref, fg_ref, e_ref, out_ref, ya_ref, yb_ref, *, dmix, sub):
    half = za_ref.shape[1]
    inv_n = 1.0 / RWKV_HEAD_DIM
    nblk = out_ref.shape[1] // MXU_DIM
    rows_of = lambda s: slice(s * sub, (s + 1) * sub)
    y_of = lambda s: (ya_ref, yb_ref)[s % 2]

    def norm_tile(y_ref, rows, i):
        sl = slice(i * LANES, (i + 1) * LANES)
        seg = _f32(o_ref[rows, sl])
        d = seg - _head_dot(seg, e_ref[...]) * inv_n
        var = _head_dot(d * d, e_ref[...]) * inv_n
        ymix = ((d * lax.rsqrt(var + RWKV_GN_EPS) * lg_ref[:, sl] + lb_ref[:, sl] + _f32(bonus_ref[rows, sl]))
                * _f32(g_ref[rows, sl]))
        z = (za_ref[rows, sl] if (i + 1) * LANES <= half
             else zb_ref[rows, i * LANES - half:(i + 1) * LANES - half])
        y_ref[:, sl] = _bf(ymix * _silu(_f32(z)))

    def pieces_of(s):
        rows = rows_of(s)
        return ([functools.partial(norm_tile, y_of(s), rows, i) for i in range(dmix // LANES)]
                + [functools.partial(_attention_head_into, y_of(s), rows, h, qm_ref, kv_ref, zb_ref,
                                     dmix - half, dmix) for h in range(XATTN_HEADS)])

    def project(s, n):
        rows, cols = rows_of(s), slice(n * MXU_DIM, (n + 1) * MXU_DIM)
        out_ref[rows, cols] = x_ref[rows, cols] + _dot(y_of(s)[...], wo_ref[:, cols])
        if n == nblk - 1:
            xn = out_ref[rows, :]
            ms = jnp.mean(xn * xn, axis=-1, keepdims=True)
            out_ref[rows, :] = xn * lax.rsqrt(ms + RMS_EPS) * fg_ref[...]

    _gate_then_project(pieces_of, project, out_ref.shape[0] // sub, nblk)


def _rwkv_out(o, g, bonus, proj, x, kv, w_out, lnx_g, lnx_b, final_g, e2, *, tm, qm_block, z_block):
    bsz, seq, dmix = o.shape
    dmodel = x.shape[-1]
    dx = XATTN_HEADS * XATTN_HEAD_DIM
    dinner = dmix + dx
    half = dinner // 2
    nmem = kv.shape[1]
    full = lambda shape: pl.BlockSpec(shape, lambda b, s: (0,) * len(shape))
    tile = lambda w, j: pl.BlockSpec((None, tm, w), lambda b, s: (b, s, j))
    return pl.pallas_call(
        functools.partial(_rwkv_out_body, dmix=dmix, sub=OUT_SUB_ROWS),
        grid=(bsz, seq // tm),
        in_specs=[tile(dmix, 0), tile(dmix, 0), tile(dmix, 0),
                  tile(dx, qm_block), tile(half, z_block), tile(half, z_block + 1), tile(dmodel, 0),
                  pl.BlockSpec((None, nmem, 2 * dx), lambda b, s: (b, 0, 0)),
                  full(w_out.shape), full((1, dmix)), full((1, dmix)), full((1, dmodel)), full(e2.shape)],
        out_specs=tile(dmodel, 0),
        out_shape=jax.ShapeDtypeStruct((bsz, seq, dmodel), F32),
        scratch_shapes=[pltpu.VMEM((OUT_SUB_ROWS, dinner), BF16)] * 2,
        compiler_params=_params("parallel", "parallel"),
        name="rwkv_out",
    )(o, g, bonus, proj, proj, proj, x, kv, w_out, lnx_g.reshape(1, dmix), lnx_b.reshape(1, dmix),
      final_g.reshape(1, dmodel), e2)


def _blockdiag_tiles(w, tile):
    nb, blk, _ = w.shape
    rows = w.reshape(nb * blk // tile, tile, blk)
    idx = jnp.arange(tile)
    on_diag = (idx[:, None] // blk) == (idx[None, :] // blk)
    return jnp.where(on_diag, jnp.tile(rows, (1, 1, tile // blk)), 0.0)


def _pad_rows(w, rows):
    return jnp.pad(w, ((0, rows - w.shape[0]), (0, 0)))


def _pad_cols(w, cols):
    return jnp.pad(w, ((0, 0), (0, cols - w.shape[1])))


def kernel(x, mem, norm_g, mem_norm_g, mem_kv_w, w_out, ml_w_in, ml_conv_w, ml_conv_b, ml_wq, ml_wk, ml_wv, ml_w_gate, ml_b_gate, ml_mhn_g, ml_skip, rw_w_in, rw_mu, rw_w_lora2, rw_w0, rw_a_lora2, rw_a0, rw_v_lora2, rw_v0, rw_g_lora2, rw_k_k, rw_k_a, rw_r_k, rw_lnx_g, rw_lnx_b, final_g):
    bsz, seq, dmodel = x.shape
    nmem = mem.shape[1]
    dmix = ml_conv_w.shape[-1]
    dx = XATTN_HEADS * XATTN_HEAD_DIM
    ntok = bsz * seq
    mem2d = mem.reshape(bsz * nmem, dmodel)

    kv0 = _rms_matmul(mem2d, mem_norm_g[0], _bf(mem_kv_w[0]), tm=512, tn=2 * dx,
                      out_dtype=BF16).reshape(bsz, nmem, 2 * dx)
    proj0 = _rms_matmul(x.reshape(ntok, dmodel), norm_g[0], _bf(ml_w_in[0]), tm=1024, tn=4096,
                        out_dtype=BF16).reshape(bsz, seq, -1)
    wqk = jnp.concatenate([_blockdiag_tiles(ml_wq[0], MXU_DIM), _blockdiag_tiles(ml_wk[0], MXU_DIM)], axis=-1)
    wv = _blockdiag_tiles(ml_wv[0], MXU_DIM)
    wg = ml_w_gate[0].reshape(3, dmix, -1)
    xc, q, k, v0, gcol, grow = _mlstm_pre(
        proj0, ml_conv_w[0], ml_conv_b[0], _bf(wqk), _bf(wv),
        _bf(jnp.pad(wg, ((0, 0), (0, 0), (0, LANES - wg.shape[-1])))),
        _bf(jnp.pad(wg.transpose(0, 2, 1), ((0, 0), (0, 16 - wg.shape[-1]), (0, 0)))),
        ml_b_gate[0], ts=512)
    hs = _mlstm_scan(q, k, v0, gcol, grow, chunk=MLSTM_CHUNK)
    x1 = _mlstm_out(hs, xc, proj0, x, kv0, _bf(w_out[0]), ml_mhn_g[0], ml_skip[0], tm=OUT_TILE_ROWS)

    w_in = rw_w_in[0]
    mu = rw_mu[0]
    cuts = [0, 3 * dmix]
    for rank in (DECAY_RANK, ICLR_RANK, VRES_RANK, GATE_RANK):
        cuts.append(cuts[-1] + rank)
    lora_w = jnp.concatenate([_pad_cols(w_in[:, cuts[i]:cuts[i + 1]], LORA_SEG) for i in range(1, 5)], axis=1)
    lora_mu = jnp.concatenate([jnp.pad(mu[cuts[i]:cuts[i + 1]], (0, LORA_SEG - (cuts[i + 1] - cuts[i])))
                               for i in range(1, 5)])
    w_packed = jnp.concatenate([w_in[:, :3 * dmix], w_in[:, cuts[-1]:], lora_w], axis=1)
    qm_block = (3 * dmix) // dx
    z_block = (3 * dmix + dx) // ((dmix + dx) // 2)
    lora_block = (3 * dmix + dx + dmix + dx) // (4 * LORA_SEG)

    kv1 = _rms_matmul(mem2d, mem_norm_g[1], _bf(mem_kv_w[1]), tm=512, tn=2 * dx,
                      out_dtype=BF16).reshape(bsz, nmem, 2 * dx)
    proj1 = _rms_matmul(x1.reshape(ntok, dmodel), norm_g[1], _bf(w_packed), tm=1024, tn=2560,
                        out_dtype=BF16).reshape(bsz, seq, -1)
    wl2 = _bf(jnp.stack([_pad_rows(rw_w_lora2[0], LORA_SEG), _pad_rows(rw_a_lora2[0], LORA_SEG),
                         _pad_rows(rw_v_lora2[0], LORA_SEG), _pad_rows(rw_g_lora2[0], LORA_SEG)]))
    vecs = jnp.stack([rw_w0[0], rw_a0[0], rw_v0[0], rw_k_k[0], rw_k_a[0], rw_r_k[0].reshape(-1),
                      jnp.zeros_like(rw_w0[0]), jnp.zeros_like(rw_w0[0])])
    lane = jnp.arange(LANES) // RWKV_HEAD_DIM
    e2 = _bf(lane[:, None] == lane[None, :])
    mus = (mu[:dmix].reshape(1, -1), mu[dmix:2 * dmix].reshape(1, -1), mu[2 * dmix:3 * dmix].reshape(1, -1),
           lora_mu.reshape(1, -1))
    r, lw, k2, v1, kk, ab, g, bonus = _rwkv_pre(proj1, v0, mus, wl2, vecs, e2, tm=256, lora_block=lora_block)
    o = _wkv7(r, lw, k2, v1, kk, ab, chunk=WKV_CHUNK)
    return _rwkv_out(o, g, bonus, proj1, x1, kv1, _bf(w_out[1]), rw_lnx_g[0], rw_lnx_b[0], final_g, e2,
                     tm=OUT_TILE_ROWS, qm_block=qm_block, z_block=z_block)
```
